```python
import math
import jax
import jax.numpy as jnp
from jax import lax
import numpy as np

D_MODEL = 1024
BATCH = 8
SEQ = 4096
DEPTH = 4

GRID_W = 64
CTX_LEN = 256
N_MIXERS = 3
D_FF = 4 * D_MODEL
EPS = 1e-6
ROPE_THETA = 10000.0
Q_BLOCK = 128
NEG_INF = -1e30

A_HEADS = 16
A_KV_HEADS = 4
A_GROUP = A_HEADS // A_KV_HEADS
A_HEAD_DIM = D_MODEL // A_HEADS
WINDOW = 128
B_HEADS = 8
B_HEAD_DIM = D_MODEL // (2 * B_HEADS)
C_HEADS = 16
C_Q_LORA = 384
C_KV_LORA = 256
C_NOPE = 64
C_ROPE = 32
C_V = 64

kernel_name = "hybrid_interleaved_diffusion_trunk"


def rms_norm(x, g):
    x32 = x.astype(jnp.float32)
    y = x32 * lax.rsqrt(jnp.mean(x32 * x32, axis=-1, keepdims=True) + EPS)
    return (y * g.astype(jnp.float32)).astype(x.dtype)


def modulate(x, shift, scale):
    return x * (1.0 + scale) + shift


def rope_2d(x, rows, cols):
    d = x.shape[-1]
    da = d // 2
    inv = ROPE_THETA ** (-jnp.arange(0, da, 2, dtype=jnp.float32) / da)
    shape = (1, x.shape[1]) + (1,) * (x.ndim - 3) + (da // 2,)

    def rot(xa, pos):
        ang = pos.astype(jnp.float32)[:, None] * inv[None, :]
        cos = jnp.cos(ang).reshape(shape).astype(x.dtype)
        sin = jnp.sin(ang).reshape(shape).astype(x.dtype)
        x1, x2 = jnp.split(xa, 2, axis=-1)
        return jnp.concatenate([x1 * cos - x2 * sin, x2 * cos + x1 * sin], axis=-1)

    return jnp.concatenate([rot(x[..., :da], rows), rot(x[..., da:], cols)], axis=-1)


def softmax_with_sink(s, sink):
    if sink is None:
        return jax.nn.softmax(s, axis=-1)
    col = jnp.broadcast_to(sink.astype(jnp.float32)[None, :, :, None, None], s.shape[:-1] + (1,))
    return jax.nn.softmax(jnp.concatenate([s, col], axis=-1), axis=-1)[..., :-1]


def dense_attention(q, k, v, scale, sink=None):
    B, Sq, G, R, dq = q.shape
    dv = v.shape[-1]
    nb = Sq // Q_BLOCK
    qb = jnp.moveaxis(q.reshape(B, nb, Q_BLOCK, G, R, dq), 1, 0)

    def block(qi):
        s = jnp.einsum("bqgrd,bkgd->bgrqk", qi, k).astype(jnp.float32) * scale
        p = softmax_with_sink(s, sink).astype(v.dtype)
        return jnp.einsum("bgrqk,bkgv->bqgrv", p, v)

    out = lax.map(block, qb)
    return jnp.moveaxis(out, 0, 1).reshape(B, Sq, G, R, dv)


def window_attention(q, k, v, kc, vc, scale, sink):
    B, S, G, R, dq = q.shape
    dv = v.shape[-1]
    nb = S // Q_BLOCK
    span = Q_BLOCK + 2 * WINDOW
    pad = ((0, 0), (WINDOW, WINDOW), (0, 0), (0, 0))
    kp = jnp.pad(k, pad)
    vp = jnp.pad(v, pad)
    qb = jnp.moveaxis(q.reshape(B, nb, Q_BLOCK, G, R, dq), 1, 0)
    offs = jnp.arange(span) - WINDOW
    band = jnp.abs(jnp.arange(Q_BLOCK)[:, None] - offs[None, :]) <= WINDOW

    def block(args):
        i, qi = args
        start = i * Q_BLOCK
        ki = lax.dynamic_slice_in_dim(kp, start, span, axis=1)
        vi = lax.dynamic_slice_in_dim(vp, start, span, axis=1)
        kpos = start + offs
        mask = band & ((kpos >= 0) & (kpos < S))[None, :]
        s_loc = jnp.einsum("bqgrd,bkgd->bgrqk", qi, ki).astype(jnp.float32) * scale
        s_loc = jnp.where(mask, s_loc, NEG_INF)
        s_ctx = jnp.einsum("bqgrd,bcgd->bgrqc", qi, kc).astype(jnp.float32) * scale
        p = softmax_with_sink(jnp.concatenate([s_loc, s_ctx], axis=-1), sink).astype(v.dtype)
        return (jnp.einsum("bgrqk,bkgv->bqgrv", p[..., :span], vi)
                + jnp.einsum("bgrqc,bcgv->bqgrv", p[..., span:], vc))

    out = lax.map(block, (jnp.arange(nb), qb))
    return jnp.moveaxis(out, 0, 1).reshape(B, S, G, R, dv)


def mixer_window_gqa(u, uc, p, rows, cols, layer_idx, need_ctx):
    B, S, _ = u.shape
    C = uc.shape[1]
    nq = A_HEADS * A_HEAD_DIM
    nk = A_KV_HEADS * A_HEAD_DIM
    scale = A_HEAD_DIM ** -0.5
    sink = p["sink"].reshape(A_KV_HEADS, A_GROUP)
    q, k, v = jnp.split(u @ p["w_qkv"], [nq, nq + nk], axis=-1)
    q = rope_2d(q.reshape(B, S, A_KV_HEADS, A_GROUP, A_HEAD_DIM), rows, cols)
    k = rope_2d(k.reshape(B, S, A_KV_HEADS, A_HEAD_DIM), rows, cols)
    v = v.reshape(B, S, A_KV_HEADS, A_HEAD_DIM)
    kc, vc = jnp.split(uc @ p["w_qkv"][:, nq:], 2, axis=-1)
    kc = kc.reshape(B, C, A_KV_HEADS, A_HEAD_DIM)
    vc = vc.reshape(B, C, A_KV_HEADS, A_HEAD_DIM)
    y = window_attention(q, k, v, kc, vc, scale, sink).reshape(B, S, nq) @ p["w_o"]
    yc = None
    if need_ctx:
        qc = (uc @ p["w_qkv"][:, :nq]).reshape(B, C, A_KV_HEADS, A_GROUP, A_HEAD_DIM)
        yc = dense_attention(qc, kc, vc, scale, sink).reshape(B, C, nq) @ p["w_o"]
    return y, yc


def mixer_diff_attention(u, uc, p, rows, cols, layer_idx, need_ctx):
    B, S, _ = u.shape
    C = uc.shape[1]
    d = B_HEAD_DIM
    nqk = B_HEADS * 2 * d
    scale = d ** -0.5
    lam_init = 0.8 - 0.6 * math.exp(-0.3 * layer_idx)
    lam = p["lambda"].astype(jnp.float32)
    lam_full = jnp.exp(jnp.sum(lam[0] * lam[1])) - jnp.exp(jnp.sum(lam[2] * lam[3])) + lam_init
    q, k, v = jnp.split(u @ p["w_qkv"], [nqk, 2 * nqk], axis=-1)
    q = rope_2d(q.reshape(B, S, B_HEADS, 2, d), rows, cols)
    k = rope_2d(k.reshape(B, S, B_HEADS, 2, d), rows, cols)
    v = v.reshape(B, S, B_HEADS, 2 * d)
    kc, vc = jnp.split(uc @ p["w_qkv"][:, nqk:], 2, axis=-1)
    kc = kc.reshape(B, C, B_HEADS, 2, d)
    vc = vc.reshape(B, C, B_HEADS, 2 * d)

    def diff_attend(qq, kk, vv):
        a1 = dense_attention(qq[:, :, :, 0:1], kk[:, :, :, 0], vv, scale)
        a2 = dense_attention(qq[:, :, :, 1:2], kk[:, :, :, 1], vv, scale)
        o = a1[:, :, :, 0] - lam_full.astype(a1.dtype) * a2[:, :, :, 0]
        o = rms_norm(o, p["subln"]) * (1.0 - lam_init)
        return o.reshape(o.shape[0], o.shape[1], -1) @ p["w_o"]

    y = diff_attend(q, jnp.concatenate([k, kc], axis=1), jnp.concatenate([v, vc], axis=1))
    yc = None
    if need_ctx:
        qc = (uc @ p["w_qkv"][:, :nqk]).reshape(B, C, B_HEADS, 2, d)
        yc = diff_attend(qc, kc, vc)
    return y, yc


def mixer_mla(u, uc, p, rows, cols, layer_idx, need_ctx):
    B, S, _ = u.shape
    C = uc.shape[1]
    dqk = C_NOPE + C_ROPE
    scale = dqk ** -0.5

    def queries(cq):
        n = cq.shape[1]
        return (rms_norm(cq, p["q_norm"]) @ p["w_uq"]).reshape(B, n, C_HEADS, dqk)

    def keys_values(ckv, kr):
        n = ckv.shape[1]
        kv = (rms_norm(ckv, p["kv_norm"]) @ p["w_ukv"]).reshape(B, n, C_HEADS, C_NOPE + C_V)
        k = jnp.concatenate([kv[..., :C_NOPE], jnp.broadcast_to(kr, (B, n, C_HEADS, C_ROPE))], axis=-1)
        return k, kv[..., C_NOPE:]

    cq, ckv, kr = jnp.split(u @ p["w_in"], [C_Q_LORA, C_Q_LORA + C_KV_LORA], axis=-1)
    q = queries(cq)
    q = jnp.concatenate([q[..., :C_NOPE], rope_2d(q[..., C_NOPE:], rows, cols)], axis=-1)
    k, v = keys_values(ckv, rope_2d(kr[:, :, None, :], rows, cols))
    ckv_c, kr_c = jnp.split(uc @ p["w_in"][:, C_Q_LORA:], [C_KV_LORA], axis=-1)
    kc, vc = keys_values(ckv_c, kr_c[:, :, None, :])
    y = dense_attention(q[:, :, :, None], jnp.concatenate([k, kc], axis=1),
                        jnp.concatenate([v, vc], axis=1), scale)
    y = y.reshape(B, S, C_HEADS * C_V) @ p["w_o"]
    yc = None
    if need_ctx:
        qc = queries(uc @ p["w_in"][:, :C_Q_LORA])
        yc = dense_attention(qc[:, :, :, None], kc, vc, scale).reshape(B, C, C_HEADS * C_V) @ p["w_o"]
    return y, yc


def squared_relu_mlp(u, w1, w2):
    h = jax.nn.relu(u @ w1)
    return (h * h) @ w2


def setup_inputs(seed: int = 0) -> dict:
    key = jax.random.key(seed)
    keys = iter(jax.random.split(key, 128))

    def normal(shape, scale=1.0):
        return jax.random.normal(next(keys), shape, jnp.float32) * scale

    inp = {
        "x": normal((BATCH, SEQ, D_MODEL)),
        "c": normal((BATCH, D_MODEL)),
        "ctx": normal((BATCH, CTX_LEN, D_MODEL)),
        "c_ctx": normal((D_MODEL,)),
    }
    for i in range(DEPTH):
        kind = i % N_MIXERS
        inp[f"l{i}_ada_w"] = normal((D_MODEL, 6 * D_MODEL), 0.5 * D_MODEL ** -0.5)
        inp[f"l{i}_ada_b"] = normal((6 * D_MODEL,), 0.02)
        inp[f"l{i}_norms"] = 1.0 + normal((4, D_MODEL), 0.05)
        if kind == 0:
            inp[f"l{i}_w_qkv"] = normal((D_MODEL, (A_HEADS + 2 * A_KV_HEADS) * A_HEAD_DIM), D_MODEL ** -0.5)
            inp[f"l{i}_sink"] = normal((A_HEADS,), 0.5)
            inp[f"l{i}_w_o"] = normal((A_HEADS * A_HEAD_DIM, D_MODEL), (A_HEADS * A_HEAD_DIM) ** -0.5)
        elif kind == 1:
            inp[f"l{i}_w_qkv"] = normal((D_MODEL, 3 * B_HEADS * 2 * B_HEAD_DIM), D_MODEL ** -0.5)
            inp[f"l{i}_lambda"] = normal((4, B_HEAD_DIM), 0.1)
            inp[f"l{i}_subln"] = 1.0 + normal((2 * B_HEAD_DIM,), 0.05)
            inp[f"l{i}_w_o"] = normal((B_HEADS * 2 * B_HEAD_DIM, D_MODEL), (B_HEADS * 2 * B_HEAD_DIM) ** -0.5)
        else:
            inp[f"l{i}_w_in"] = normal((D_MODEL, C_Q_LORA + C_KV_LORA + C_ROPE), D_MODEL ** -0.5)
            inp[f"l{i}_q_norm"] = 1.0 + normal((C_Q_LORA,), 0.05)
            inp[f"l{i}_kv_norm"] = 1.0 + normal((C_KV_LORA,), 0.05)
            inp[f"l{i}_w_uq"] = normal((C_Q_LORA, C_HEADS * (C_NOPE + C_ROPE)), C_Q_LORA ** -0.5)
            inp[f"l{i}_w_ukv"] = normal((C_KV_LORA, C_HEADS * (C_NOPE + C_V)), C_KV_LORA ** -0.5)
            inp[f"l{i}_w_o"] = normal((C_HEADS * C_V, D_MODEL), (C_HEADS * C_V) ** -0.5)
        inp[f"l{i}_mlp_w1"] = normal((D_MODEL, D_FF), D_MODEL ** -0.5)
        inp[f"l{i}_mlp_w2"] = normal((D_FF, D_MODEL), D_FF ** -0.5)
    return inp


def reference(x, c, ctx, c_ctx,
              l0_ada_w, l0_ada_b, l0_norms, l0_w_qkv, l0_sink, l0_w_o, l0_mlp_w1, l0_mlp_w2,
              l1_ada_w, l1_ada_b, l1_norms, l1_w_qkv, l1_lambda, l1_subln, l1_w_o, l1_mlp_w1, l1_mlp_w2,
              l2_ada_w, l2_ada_b, l2_norms, l2_w_in, l2_q_norm, l2_kv_norm, l2_w_uq, l2_w_ukv, l2_w_o,
              l2_mlp_w1, l2_mlp_w2,
              l3_ada_w, l3_ada_b, l3_norms, l3_w_qkv, l3_sink, l3_w_o, l3_mlp_w1, l3_mlp_w2):
    B, S, D = x.shape
    ROWS = S // GRID_W
    rows = jnp.repeat(jnp.arange(ROWS, dtype=jnp.int32), GRID_W)
    cols = jnp.tile(jnp.arange(GRID_W, dtype=jnp.int32), ROWS)

    layers = [
        dict(ada_w=l0_ada_w, ada_b=l0_ada_b, norms=l0_norms, w_qkv=l0_w_qkv, sink=l0_sink, w_o=l0_w_o,
             mlp_w1=l0_mlp_w1, mlp_w2=l0_mlp_w2),
        dict(ada_w=l1_ada_w, ada_b=l1_ada_b, norms=l1_norms, w_qkv=l1_w_qkv, subln=l1_subln, w_o=l1_w_o,
             mlp_w1=l1_mlp_w1, mlp_w2=l1_mlp_w2, **{"lambda": l1_lambda}),
        dict(ada_w=l2_ada_w, ada_b=l2_ada_b, norms=l2_norms, w_in=l2_w_in, q_norm=l2_q_norm,
             kv_norm=l2_kv_norm, w_uq=l2_w_uq, w_ukv=l2_w_ukv, w_o=l2_w_o,
             mlp_w1=l2_mlp_w1, mlp_w2=l2_mlp_w2),
        dict(ada_w=l3_ada_w, ada_b=l3_ada_b, norms=l3_norms, w_qkv=l3_w_qkv, sink=l3_sink, w_o=l3_w_o,
             mlp_w1=l3_mlp_w1, mlp_w2=l3_mlp_w2),
    ]
    mixers = (mixer_window_gqa, mixer_diff_attention, mixer_mla)

    sc = jax.nn.silu(c)
    scc = jax.nn.silu(c_ctx)
    h, hc = x, ctx
    for i in range(DEPTH):
        p = layers[i]
        last = i == DEPTH - 1
        g = p["norms"]
        mod = jnp.split((sc @ p["ada_w"] + p["ada_b"])[:, None, :], 6, axis=-1)
        n_mod_c = 2 if last else 6
        mod_c = jnp.split(scc @ p["ada_w"][:, :n_mod_c * D] + p["ada_b"][:n_mod_c * D], n_mod_c, axis=-1)

        u = modulate(rms_norm(h, g[0]), mod[0], mod[1])
        uc = modulate(rms_norm(hc, g[0]), mod_c[0], mod_c[1])
        y, yc = mixers[i % N_MIXERS](u, uc, p, rows, cols, i, not last)
        h = h + mod[2] * rms_norm(y, g[1])
        u = modulate(rms_norm(h, g[2]), mod[3], mod[4])
        h = h + mod[5] * rms_norm(squared_relu_mlp(u, p["mlp_w1"], p["mlp_w2"]), g[3])
        if not last:
            hc = hc + mod_c[2] * rms_norm(yc, g[1])
            uc = modulate(rms_norm(hc, g[2]), mod_c[3], mod_c[4])
            hc = hc + mod_c[5] * rms_norm(squared_relu_mlp(uc, p["mlp_w1"], p["mlp_w2"]), g[3])
    return h
```

```python
import functools
import math

import jax
import jax.numpy as jnp
import numpy as np
from jax import lax
from jax.experimental import pallas as pl
from jax.experimental.pallas import tpu as pltpu

D_MODEL = 1024
SEQ = 4096
CTX_LEN = 256
S_ALL = SEQ + CTX_LEN
GRID_W = 64
D_FF = 4 * D_MODEL
EPS = 1e-6
ROPE_THETA = 10000.0
NEG_INF = -1e30
WINDOW = 128

A_HEADS = 16
A_KV_HEADS = 4
A_GROUP = 4
A_HEAD_DIM = 64
B_HEADS = 8
B_HEAD_DIM = 64
C_HEADS = 16
C_Q_LORA = 384
C_KV_LORA = 256
C_NOPE = 64
C_ROPE = 32
C_V = 64

LANES = 128
MXU_W = 256
TOK_TILE = 256
TILES_PER_SAMPLE = S_ALL // TOK_TILE
LATENT_TILES = SEQ // TOK_TILE
KV_CHUNK = 512
VMEM_LIMIT = 56 * 1024 * 1024

_BF16 = jnp.bfloat16
_F32 = jnp.float32


def _params(n_parallel):
    return pltpu.CompilerParams(
        dimension_semantics=("parallel",) * n_parallel, vmem_limit_bytes=VMEM_LIMIT)


def _rms(x):
    return x * lax.rsqrt(jnp.mean(x * x, axis=-1, keepdims=True) + EPS)


def _dot(a, b):
    return jnp.dot(a, b, preferred_element_type=_F32)


def _dot_nt(a, b):
    return lax.dot_general(a, b, (((1,), (1,)), ((), ())), preferred_element_type=_F32)


def _rope_tables(d_rot, lane_off, period):
    q = d_rot // 4
    da = d_rot // 2
    inv = ROPE_THETA ** (-jnp.arange(0, da, 2, dtype=_F32) / da)
    t = jnp.arange(SEQ, dtype=jnp.int32)
    rows = (t // GRID_W).astype(_F32)[:, None] * inv[None, :]
    cols = (t % GRID_W).astype(_F32)[:, None] * inv[None, :]
    zero = jnp.zeros((SEQ, q), _F32)
    cos = jnp.concatenate([jnp.cos(rows), jnp.cos(rows), jnp.cos(cols), jnp.cos(cols)], axis=1)
    sin_up = jnp.concatenate([-jnp.sin(rows), zero, -jnp.sin(cols), zero], axis=1)
    sin_dn = jnp.concatenate([zero, jnp.sin(rows), zero, jnp.sin(cols)], axis=1)

    def place(tab, fill):
        unit = jnp.full((SEQ, period), fill, _F32).at[:, lane_off:lane_off + d_rot].set(tab)
        full = jnp.tile(unit, (1, MXU_W // period))
        ctx = jnp.full((CTX_LEN, MXU_W), fill, _F32)
        return jnp.concatenate([full, ctx], axis=0)

    return place(cos, 1.0), place(sin_up, 0.0), place(sin_dn, 0.0)


def _rope_block(x, cos, sin_up, sin_dn, q):
    width = x.shape[1]
    up = pltpu.roll(x, width - q, 1)
    dn = pltpu.roll(x, q, 1)
    return x * cos + up * sin_up + dn * sin_dn


def _ada_kernel(c_ref, w_ref, b_ref, o_ref):
    c = c_ref[...]
    s = c * (1.0 / (1.0 + jnp.exp(-c)))
    o_ref[...] = jnp.dot(s, w_ref[...], preferred_element_type=_F32,
                         precision=lax.Precision.HIGHEST) + b_ref[...]


def _ada(cc, w, b):
    n = w.shape[1]
    bn = D_MODEL
    return pl.pallas_call(
        _ada_kernel,
        grid=(n // bn,),
        in_specs=[pl.BlockSpec((16, D_MODEL), lambda j: (0, 0)),
                  pl.BlockSpec((D_MODEL, bn), lambda j: (0, j)),
                  pl.BlockSpec((1, bn), lambda j: (0, j))],
        out_specs=pl.BlockSpec((16, bn), lambda j: (0, j)),
        out_shape=jax.ShapeDtypeStruct((16, n), _F32),
        compiler_params=_params(1),
        name="ada",
    )(cc, w, b.reshape(1, n))


def _modulated(h_ref, mod_ref, g_ref, norm_row, shift_row):
    y = _rms(h_ref[...]) * g_ref[norm_row:norm_row + 1, :]
    return y * (1.0 + mod_ref[shift_row + 1:shift_row + 2, :]) + mod_ref[shift_row:shift_row + 1, :]


def _tok_spec(width):
    return pl.BlockSpec((None, TOK_TILE, width), lambda b, j: (b, j, 0))


def _mod_spec():
    return pl.BlockSpec((None, None, 6, D_MODEL), lambda b, j: (b, j // LATENT_TILES, 0, 0))


def _const_spec(shape):
    return pl.BlockSpec(shape, lambda b, j: (0,) * len(shape))


def _table_spec():
    return pl.BlockSpec((TOK_TILE, MXU_W), lambda b, j: (j, 0))


def _qkv_kernel(h_ref, mod_ref, g_ref, w_ref, cos_ref, up_ref, dn_ref, q_ref, k_ref, v_ref,
                *, nq, nk, qscale):
    u = _modulated(h_ref, mod_ref, g_ref, 0, 0).astype(_BF16)
    cos, up, dn = cos_ref[...], up_ref[...], dn_ref[...]
    quarter = A_HEAD_DIM // 4
    for blk in range(nq // MXU_W):
        x = _dot(u, w_ref[:, blk * MXU_W:(blk + 1) * MXU_W])
        x = _rope_block(x, cos, up, dn, quarter) * qscale
        q_ref[:, blk * MXU_W:(blk + 1) * MXU_W] = x.astype(_BF16)
    for blk in range(nk // MXU_W):
        x = _dot(u, w_ref[:, nq + blk * MXU_W:nq + (blk + 1) * MXU_W])
        k_ref[:, blk * MXU_W:(blk + 1) * MXU_W] = _rope_block(x, cos, up, dn, quarter).astype(_BF16)
    v_ref[...] = _dot(u, w_ref[:, nq + nk:]).astype(_BF16)


def _qkv_proj(h, mod_all, norms, w, tables, nq, nk, nv, qscale):
    batch = h.shape[0]
    return pl.pallas_call(
        functools.partial(_qkv_kernel, nq=nq, nk=nk, qscale=qscale),
        grid=(batch, TILES_PER_SAMPLE),
        in_specs=[_tok_spec(D_MODEL), _mod_spec(), _const_spec((4, D_MODEL)),
                  _const_spec((D_MODEL, nq + nk + nv)),
                  _table_spec(), _table_spec(), _table_spec()],
        out_specs=[_tok_spec(nq), _tok_spec(nk), _tok_spec(nv)],
        out_shape=[jax.ShapeDtypeStruct((batch, S_ALL, n), _BF16) for n in (nq, nk, nv)],
        compiler_params=_params(2),
        name="qkv_proj",
    )(h, mod_all, norms, w, *tables)


C_QK_PAD = LANES
C_IN_PAD = C_Q_LORA + C_KV_LORA + LANES


def _mla_proj_kernel(h_ref, mod_ref, g_ref, w_in_ref, qn_ref, kvn_ref, w_uq_ref, w_uk_ref, w_uv_ref,
                     cos_ref, up_ref, dn_ref, q_ref, k_ref, v_ref, *, qscale):
    u = _modulated(h_ref, mod_ref, g_ref, 0, 0).astype(_BF16)
    cos, up, dn = cos_ref[...], up_ref[...], dn_ref[...]
    quarter = C_ROPE // 4
    t = _dot(u, w_in_ref[...])
    cq = (_rms(t[:, :C_Q_LORA]) * qn_ref[...]).astype(_BF16)
    ckv = (_rms(t[:, C_Q_LORA:C_Q_LORA + C_KV_LORA]) * kvn_ref[...]).astype(_BF16)
    kr = _rope_block(t[:, C_Q_LORA + C_KV_LORA:], cos[:, :LANES], up[:, :LANES], dn[:, :LANES], quarter)
    kr = jnp.concatenate([kr] * (MXU_W // C_QK_PAD), axis=1)
    for blk in range(C_HEADS * C_QK_PAD // MXU_W):
        sl = slice(blk * MXU_W, (blk + 1) * MXU_W)
        q = _rope_block(_dot(cq, w_uq_ref[:, sl]), cos, up, dn, quarter) * qscale
        q_ref[:, sl] = q.astype(_BF16)
        k_ref[:, sl] = (_dot(ckv, w_uk_ref[:, sl]) + kr).astype(_BF16)
    v_ref[...] = _dot(ckv, w_uv_ref[...]).astype(_BF16)


def _mla_proj(h, mod_all, norms, w_in, q_norm, kv_norm, w_uq, w_uk, w_uv, tables, qscale):
    batch = h.shape[0]
    nqk = C_HEADS * C_QK_PAD
    nv = C_HEADS * C_V
    return pl.pallas_call(
        functools.partial(_mla_proj_kernel, qscale=qscale),
        grid=(batch, TILES_PER_SAMPLE),
        in_specs=[_tok_spec(D_MODEL), _mod_spec(), _const_spec((4, D_MODEL)),
                  _const_spec((D_MODEL, C_IN_PAD)), _const_spec((1, C_Q_LORA)),
                  _const_spec((1, C_KV_LORA)), _const_spec((C_Q_LORA, nqk)),
                  _const_spec((C_KV_LORA, nqk)), _const_spec((C_KV_LORA, nv)),
                  _table_spec(), _table_spec(), _table_spec()],
        out_specs=[_tok_spec(nqk), _tok_spec(nqk), _tok_spec(nv)],
        out_shape=[jax.ShapeDtypeStruct((batch, S_ALL, n), _BF16) for n in (nqk, nqk, nv)],
        compiler_params=_params(2),
        name="mla_proj",
    )(h, mod_all, norms, w_in, q_norm, kv_norm, w_uq, w_uk, w_uv, *tables)


A_BLOCK = A_KV_HEADS * A_HEAD_DIM
A_KEYS = 2 * TOK_TILE + CTX_LEN


def _window_kernel(sink_ref, q_ref, kp_ref, kc_ref, kn_ref, kx_ref, vp_ref, vc_ref, vn_ref, vx_ref,
                   o_ref):
    j = pl.program_id(1)
    k_all = jnp.concatenate([kp_ref[...], kc_ref[...], kn_ref[...], kx_ref[...]], axis=0)
    v_all = jnp.concatenate([vp_ref[...], vc_ref[...], vn_ref[...], vx_ref[...]], axis=0)
    qi = lax.broadcasted_iota(jnp.int32, (TOK_TILE, A_KEYS), 0)
    c = lax.broadcasted_iota(jnp.int32, (TOK_TILE, A_KEYS), 1)
    kpos = TOK_TILE * j - WINDOW + c
    local = ((c - qi >= 0) & (c - qi <= 2 * WINDOW) & (kpos >= 0) & (kpos < SEQ)
             & (j < LATENT_TILES))
    mask = jnp.concatenate([local | (c >= 2 * TOK_TILE)] * A_GROUP, axis=0)
    lane_group = lax.broadcasted_iota(jnp.int32, (TOK_TILE, A_BLOCK), 1) // A_HEAD_DIM
    out = [jnp.zeros((TOK_TILE, A_BLOCK), _F32) for _ in range(A_GROUP)]
    for g in range(A_KV_HEADS):
        sel = lane_group == g
        qg = jnp.concatenate(
            [jnp.where(sel, q_ref[:, r * A_BLOCK:(r + 1) * A_BLOCK], jnp.zeros((), _BF16))
             for r in range(A_GROUP)], axis=0)
        s = jnp.where(mask, _dot_nt(qg, k_all), NEG_INF)
        sink = jnp.concatenate(
            [jnp.full((TOK_TILE, 1), sink_ref[g * A_GROUP + r], _F32) for r in range(A_GROUP)], axis=0)
        m = jnp.maximum(jnp.max(s, axis=-1, keepdims=True), sink)
        p = jnp.exp(s - m)
        denom = jnp.sum(p, axis=-1, keepdims=True) + jnp.exp(sink - m)
        og = _dot(p.astype(_BF16), v_all) / denom
        for r in range(A_GROUP):
            out[r] = jnp.where(sel, og[r * TOK_TILE:(r + 1) * TOK_TILE], out[r])
    for r in range(A_GROUP):
        o_ref[:, r * A_BLOCK:(r + 1) * A_BLOCK] = out[r].astype(_BF16)


def _window_attention(q, k, v, sink, n_tiles):
    batch = q.shape[0]
    half = TOK_TILE // 2
    last_half = S_ALL // half - 1

    def half_spec(fn):
        return pl.BlockSpec((None, half, A_BLOCK), fn)

    prev_spec = half_spec(lambda b, j: (b, jnp.maximum(2 * j - 1, 0), 0))
    next_spec = half_spec(lambda b, j: (b, jnp.minimum(2 * j + 2, last_half), 0))
    cur_spec = pl.BlockSpec((None, TOK_TILE, A_BLOCK), lambda b, j: (b, j, 0))
    ctx_spec = pl.BlockSpec((None, CTX_LEN, A_BLOCK), lambda b, j: (b, LATENT_TILES, 0))
    kv_specs = [prev_spec, cur_spec, next_spec, ctx_spec]
    return pl.pallas_call(
        _window_kernel,
        grid=(batch, n_tiles),
        in_specs=[pl.BlockSpec(memory_space=pltpu.SMEM), _tok_spec(D_MODEL)] + kv_specs + kv_specs,
        out_specs=_tok_spec(D_MODEL),
        out_shape=jax.ShapeDtypeStruct((batch, n_tiles * TOK_TILE, D_MODEL), _BF16),
        compiler_params=_params(2),
        name="window_attention",
    )(sink, q, k, k, k, k, v, v, v, v)


def _online_softmax(q, k_ref, v_ref, m_ref, l_ref, acc_ref, skip_latent):
    s = _dot_nt(q, k_ref[SEQ:, :])
    m = jnp.max(s, axis=-1, keepdims=True)
    p = jnp.exp(s - m)
    m_ref[...] = m
    l_ref[...] = jnp.sum(p, axis=-1, keepdims=True)
    acc_ref[...] = _dot(p.astype(_BF16), v_ref[SEQ:, :])

    def body(i, carry):
        start = pl.multiple_of(i * KV_CHUNK, KV_CHUNK)
        s = _dot_nt(q, k_ref[pl.ds(start, KV_CHUNK), :])
        m_old = m_ref[...]
        m_new = jnp.maximum(m_old, jnp.max(s, axis=-1, keepdims=True))
        alpha = jnp.exp(m_old - m_new)
        p = jnp.exp(s - m_new)
        l_ref[...] = alpha * l_ref[...] + jnp.sum(p, axis=-1, keepdims=True)
        acc_ref[...] = alpha * acc_ref[...] + _dot(p.astype(_BF16), v_ref[pl.ds(start, KV_CHUNK), :])
        m_ref[...] = m_new
        return carry

    lax.fori_loop(0, jnp.where(skip_latent, 0, SEQ // KV_CHUNK), body, 0)
    return acc_ref[...] / l_ref[...]


def _diff_kernel(lam_ref, subln_ref, q_ref, k_ref, v_ref, o_ref, m_ref, l_ref, acc_ref, *, lam_init):
    lam = lam_ref[...]
    lam_full = (jnp.exp(jnp.sum(lam[0:1] * lam[1:2], axis=-1, keepdims=True))
                - jnp.exp(jnp.sum(lam[2:3] * lam[3:4], axis=-1, keepdims=True)) + lam_init)
    q = q_ref[...]
    first = lax.broadcasted_iota(jnp.int32, q.shape, 1) < B_HEAD_DIM
    zero = jnp.zeros((), _BF16)
    q2 = jnp.concatenate([jnp.where(first, q, zero), jnp.where(first, zero, q)], axis=0)
    a = _online_softmax(q2, k_ref, v_ref, m_ref, l_ref, acc_ref, pl.program_id(2) == LATENT_TILES)
    o = a[:TOK_TILE] - lam_full * a[TOK_TILE:]
    o = _rms(o) * subln_ref[...] * (1.0 - lam_init)
    o_ref[...] = o.astype(_BF16)


def _diff_attention(q, k, v, lam, subln, lam_init, n_tiles):
    batch = q.shape[0]
    head_w = 2 * B_HEAD_DIM
    q_spec = pl.BlockSpec((None, TOK_TILE, head_w), lambda b, h, j: (b, j, h))
    kv_spec = pl.BlockSpec((None, S_ALL, head_w), lambda b, h, j: (b, 0, h))
    rows = 2 * TOK_TILE
    return pl.pallas_call(
        functools.partial(_diff_kernel, lam_init=lam_init),
        grid=(batch, B_HEADS, n_tiles),
        in_specs=[pl.BlockSpec((4, B_HEAD_DIM), lambda b, h, j: (0, 0)),
                  pl.BlockSpec((1, head_w), lambda b, h, j: (0, 0)),
                  q_spec, kv_spec, kv_spec],
        out_specs=q_spec,
        out_shape=jax.ShapeDtypeStruct((batch, n_tiles * TOK_TILE, B_HEADS * head_w), _BF16),
        scratch_shapes=[pltpu.VMEM((rows, 1), _F32), pltpu.VMEM((rows, 1), _F32),
                        pltpu.VMEM((rows, head_w), _F32)],
        compiler_params=_params(3),
        name="diff_attention",
    )(lam, subln, q, k, v)


def _mla_kernel(q_ref, k_ref, v_ref, o_ref, m_ref, l_ref, acc_ref):
    q = q_ref[...]
    first = lax.broadcasted_iota(jnp.int32, q.shape, 1) < C_QK_PAD
    zero = jnp.zeros((), _BF16)
    q2 = jnp.concatenate([jnp.where(first, q, zero), jnp.where(first, zero, q)], axis=0)
    a = _online_softmax(q2, k_ref, v_ref, m_ref, l_ref, acc_ref, pl.program_id(2) == LATENT_TILES)
    first_v = lax.broadcasted_iota(jnp.int32, (TOK_TILE, 2 * C_V), 1) < C_V
    o_ref[...] = jnp.where(first_v, a[:TOK_TILE], a[TOK_TILE:]).astype(_BF16)


def _mla_attention(q, k, v, n_tiles):
    batch = q.shape[0]
    pairs = C_HEADS // 2
    q_spec = pl.BlockSpec((None, TOK_TILE, 2 * C_QK_PAD), lambda b, h, j: (b, j, h))
    k_spec = pl.BlockSpec((None, S_ALL, 2 * C_QK_PAD), lambda b, h, j: (b, 0, h))
    v_spec = pl.BlockSpec((None, S_ALL, 2 * C_V), lambda b, h, j: (b, 0, h))
    o_spec = pl.BlockSpec((None, TOK_TILE, 2 * C_V), lambda b, h, j: (b, j, h))
    rows = 2 * TOK_TILE
    return pl.pallas_call(
        _mla_kernel,
        grid=(batch, pairs, n_tiles),
        in_specs=[q_spec, k_spec, v_spec],
        out_specs=o_spec,
        out_shape=jax.ShapeDtypeStruct((batch, n_tiles * TOK_TILE, C_HEADS * C_V), _BF16),
        scratch_shapes=[pltpu.VMEM((rows, 1), _F32), pltpu.VMEM((rows, 1), _F32),
                        pltpu.VMEM((rows, 2 * C_V), _F32)],
        compiler_params=_params(3),
        name="mla_attention",
    )(q, k, v)


def _post_kernel(a_ref, h_ref, mod_ref, g_ref, wo_ref, w1_ref, w2_ref, o_ref):
    y = _dot(a_ref[...], wo_ref[...])
    h1 = h_ref[...] + mod_ref[2:3, :] * (_rms(y) * g_ref[1:2, :])
    u = (_rms(h1) * g_ref[2:3, :]) * (1.0 + mod_ref[4:5, :]) + mod_ref[3:4, :]
    t = jnp.maximum(_dot(u.astype(_BF16), w1_ref[...]), 0.0)
    z = _dot((t * t).astype(_BF16), w2_ref[...])
    o_ref[...] = h1 + mod_ref[5:6, :] * (_rms(z) * g_ref[3:4, :])


def _post(a, h, mod_all, norms, w_o, w1, w2, n_tiles):
    batch = h.shape[0]
    return pl.pallas_call(
        _post_kernel,
        grid=(batch, n_tiles),
        in_specs=[_tok_spec(D_MODEL), _tok_spec(D_MODEL), _mod_spec(), _const_spec((4, D_MODEL)),
                  _const_spec((D_MODEL, D_MODEL)), _const_spec((D_MODEL, D_FF)),
                  _const_spec((D_FF, D_MODEL))],
        out_specs=_tok_spec(D_MODEL),
        out_shape=jax.ShapeDtypeStruct((batch, n_tiles * TOK_TILE, D_MODEL), _F32),
        compiler_params=_params(2),
        name="post",
    )(a, h, mod_all, norms, w_o, w1, w2)


def _a_head_perm():
    new = np.arange(A_HEADS * A_HEAD_DIM)
    r, g, d = new // A_BLOCK, (new % A_BLOCK) // A_HEAD_DIM, new % A_HEAD_DIM
    return (g * A_GROUP + r) * A_HEAD_DIM + d


def _layer_window(h, mod_all, norms, w_qkv, sink, w_o, n_tiles, tables):
    nq = A_HEADS * A_HEAD_DIM
    nk = A_KV_HEADS * A_HEAD_DIM
    perm = _a_head_perm()
    w = jnp.concatenate([w_qkv[:, :nq][:, perm], w_qkv[:, nq:]], axis=1).astype(_BF16)
    q, k, v = _qkv_proj(h, mod_all, norms, w, tables, nq, nk, nk, A_HEAD_DIM ** -0.5)
    a = _window_attention(q, k, v, sink, n_tiles)
    return a, w_o[perm, :]


def _layer_diff(h, mod_all, norms, w_qkv, lam, subln, w_o, layer_idx, n_tiles, tables):
    n = B_HEADS * 2 * B_HEAD_DIM
    lam_init = 0.8 - 0.6 * math.exp(-0.3 * layer_idx)
    q, k, v = _qkv_proj(h, mod_all, norms, w_qkv.astype(_BF16), tables, n, n, n, B_HEAD_DIM ** -0.5)
    a = _diff_attention(q, k, v, lam, subln.reshape(1, -1), lam_init, n_tiles)
    return a, w_o


def _layer_mla(h, mod_all, norms, w_in, q_norm, kv_norm, w_uq, w_ukv, w_o, n_tiles, tables):
    dqk = C_NOPE + C_ROPE
    zeros = functools.partial(jnp.zeros, dtype=_F32)
    w_in_p = jnp.concatenate(
        [w_in[:, :C_Q_LORA + C_KV_LORA], zeros((D_MODEL, C_NOPE)), w_in[:, C_Q_LORA + C_KV_LORA:],
         zeros((D_MODEL, C_QK_PAD - dqk))], axis=1).astype(_BF16)
    w_uq_p = jnp.pad(w_uq.reshape(C_Q_LORA, C_HEADS, dqk), ((0, 0), (0, 0), (0, C_QK_PAD - dqk)))
    w_uq_p = w_uq_p.reshape(C_Q_LORA, C_HEADS * C_QK_PAD).astype(_BF16)
    w_ukv_h = w_ukv.reshape(C_KV_LORA, C_HEADS, C_NOPE + C_V)
    w_uk_p = jnp.pad(w_ukv_h[:, :, :C_NOPE], ((0, 0), (0, 0), (0, C_QK_PAD - C_NOPE)))
    w_uk_p = w_uk_p.reshape(C_KV_LORA, C_HEADS * C_QK_PAD).astype(_BF16)
    w_uv = w_ukv_h[:, :, C_NOPE:].reshape(C_KV_LORA, C_HEADS * C_V).astype(_BF16)
    q, k, v = _mla_proj(h, mod_all, norms, w_in_p, q_norm.reshape(1, -1), kv_norm.reshape(1, -1),
                        w_uq_p, w_uk_p, w_uv, tables, dqk ** -0.5)
    a = _mla_attention(q, k, v, n_tiles)
    return a, w_o


def kernel(x, c, ctx, c_ctx, l0_ada_w, l0_ada_b, l0_norms, l0_w_qkv, l0_sink, l0_w_o, l0_mlp_w1, l0_mlp_w2, l1_ada_w, l1_ada_b, l1_norms, l1_w_qkv, l1_lambda, l1_subln, l1_w_o, l1_mlp_w1, l1_mlp_w2, l2_ada_w, l2_ada_b, l2_norms, l2_w_in, l2_q_norm, l2_kv_norm, l2_w_uq, l2_w_ukv, l2_w_o, l2_mlp_w1, l2_mlp_w2, l3_ada_w, l3_ada_b, l3_norms, l3_w_qkv, l3_sink, l3_w_o, l3_mlp_w1, l3_mlp_w2):
    batch = x.shape[0]
    assert x.shape == (batch, SEQ, D_MODEL) and ctx.shape == (batch, CTX_LEN, D_MODEL) and batch < 16
    layers = [
        (l0_ada_w, l0_ada_b, l0_norms, l0_w_o, l0_mlp_w1, l0_mlp_w2),
        (l1_ada_w, l1_ada_b, l1_norms, l1_w_o, l1_mlp_w1, l1_mlp_w2),
        (l2_ada_w, l2_ada_b, l2_norms, l2_w_o, l2_mlp_w1, l2_mlp_w2),
        (l3_ada_w, l3_ada_b, l3_norms, l3_w_o, l3_mlp_w1, l3_mlp_w2),
    ]
    tables_ab = _rope_tables(A_HEAD_DIM, 0, A_HEAD_DIM)
    tables_c = _rope_tables(C_ROPE, C_NOPE, LANES)
    cc = jnp.zeros((16, D_MODEL), _F32).at[:batch].set(c).at[batch].set(c_ctx)
    h = jnp.concatenate([x, ctx], axis=1)
    depth = len(layers)
    for i, (ada_w, ada_b, norms, w_o, w1, w2) in enumerate(layers):
        n_tiles = LATENT_TILES if i == depth - 1 else TILES_PER_SAMPLE
        mod = _ada(cc, ada_w, ada_b)
        mod_lat = mod[:batch].reshape(batch, 1, 6, D_MODEL)
        mod_ctx = jnp.broadcast_to(mod[batch].reshape(1, 1, 6, D_MODEL), (batch, 1, 6, D_MODEL))
        mod_all = jnp.concatenate([mod_lat, mod_ctx], axis=1)
        if i == 0:
            a, w_o = _layer_window(h, mod_all, norms, l0_w_qkv, l0_sink, w_o, n_tiles, tables_ab)
        elif i == 1:
            a, w_o = _layer_diff(h, mod_all, norms, l1_w_qkv, l1_lambda, l1_subln, w_o, i, n_tiles,
                                 tables_ab)
        elif i == 2:
            a, w_o = _layer_mla(h, mod_all, norms, l2_w_in, l2_q_norm, l2_kv_norm, l2_w_uq, l2_w_ukv,
                                w_o, n_tiles, tables_c)
        else:
            a, w_o = _layer_window(h, mod_all, norms, l3_w_qkv, l3_sink, w_o, n_tiles, tables_ab)
        h = _post(a, h, mod_all, norms, w_o.astype(_BF16), w1.astype(_BF16), w2.astype(_BF16), n_tiles)
    return h
```

```python
import functools
import math

import jax
import jax.numpy as jnp
import numpy as np
from jax import lax
from jax.experimental import pallas as pl
from jax.experimental.pallas import tpu as pltpu

D_MODEL = 1024
SEQ = 4096
CTX_LEN = 256
S_ALL = SEQ + CTX_LEN
GRID_W = 64
D_FF = 4 * D_MODEL
EPS = 1e-6
ROPE_THETA = 10000.0
NEG_INF = -1e30
WINDOW = 128
LOG2E = math.log2(math.e)

A_HEADS = 16
A_KV_HEADS = 4
A_GROUP = 4
A_HEAD_DIM = 64
B_HEADS = 8
B_HEAD_DIM = 64
C_HEADS = 16
C_Q_LORA = 384
C_KV_LORA = 256
C_NOPE = 64
C_ROPE = 32
C_V = 64

LANES = 128
MXU_W = 256
TOK_TILE = 256
TILES_PER_SAMPLE = S_ALL // TOK_TILE
LATENT_TILES = SEQ // TOK_TILE
SOFTMAX_ROWS = 16
PARTIALS = 2
SOFTMAX_UNROLL = 4
VMEM_LIMIT = 56 * 1024 * 1024

_BF16 = jnp.bfloat16
_F32 = jnp.float32


def _params(n_parallel):
    return pltpu.CompilerParams(
        dimension_semantics=("parallel",) * n_parallel, vmem_limit_bytes=VMEM_LIMIT)


def _rms(x):
    return x * lax.rsqrt(jnp.mean(x * x, axis=-1, keepdims=True) + EPS)


def _dot(a, b):
    return jnp.dot(a, b, preferred_element_type=_F32)


def _dot_nt(a, b):
    return lax.dot_general(a, b, (((1,), (1,)), ((), ())), preferred_element_type=_F32)


def _rope_tables(d_rot, lane_off, period):
    q = d_rot // 4
    da = d_rot // 2
    inv = ROPE_THETA ** (-jnp.arange(0, da, 2, dtype=_F32) / da)
    t = jnp.arange(SEQ, dtype=jnp.int32)
    rows = (t // GRID_W).astype(_F32)[:, None] * inv[None, :]
    cols = (t % GRID_W).astype(_F32)[:, None] * inv[None, :]
    zero = jnp.zeros((SEQ, q), _F32)
    cos = jnp.concatenate([jnp.cos(rows), jnp.cos(rows), jnp.cos(cols), jnp.cos(cols)], axis=1)
    sin_up = jnp.concatenate([-jnp.sin(rows), zero, -jnp.sin(cols), zero], axis=1)
    sin_dn = jnp.concatenate([zero, jnp.sin(rows), zero, jnp.sin(cols)], axis=1)

    def place(tab, fill):
        unit = jnp.full((SEQ, period), fill, _F32).at[:, lane_off:lane_off + d_rot].set(tab)
        full = jnp.tile(unit, (1, MXU_W // period))
        ctx = jnp.full((CTX_LEN, MXU_W), fill, _F32)
        return jnp.concatenate([full, ctx], axis=0)

    return place(cos, 1.0), place(sin_up, 0.0), place(sin_dn, 0.0)


def _rope_block(x, cos, sin_up, sin_dn, q):
    width = x.shape[1]
    up = pltpu.roll(x, width - q, 1)
    dn = pltpu.roll(x, q, 1)
    return x * cos + up * sin_up + dn * sin_dn


def _ada_kernel(c_ref, w_ref, b_ref, o_ref):
    c = c_ref[...]
    s = c * (1.0 / (1.0 + jnp.exp(-c)))
    o_ref[...] = jnp.dot(s, w_ref[...], preferred_element_type=_F32,
                         precision=lax.Precision.HIGHEST) + b_ref[...]


def _ada(cc, w, b):
    n = w.shape[1]
    bn = D_MODEL
    return pl.pallas_call(
        _ada_kernel,
        grid=(n // bn,),
        in_specs=[pl.BlockSpec((16, D_MODEL), lambda j: (0, 0)),
                  pl.BlockSpec((D_MODEL, bn), lambda j: (0, j)),
                  pl.BlockSpec((1, bn), lambda j: (0, j))],
        out_specs=pl.BlockSpec((16, bn), lambda j: (0, j)),
        out_shape=jax.ShapeDtypeStruct((16, n), _F32),
        compiler_params=_params(1),
        name="ada",
    )(cc, w, b.reshape(1, n))


def _modulated(h_ref, mod_ref, g_ref, norm_row, shift_row):
    y = _rms(h_ref[...]) * g_ref[norm_row:norm_row + 1, :]
    return y * (1.0 + mod_ref[shift_row + 1:shift_row + 2, :]) + mod_ref[shift_row:shift_row + 1, :]


def _tok_spec(width):
    return pl.BlockSpec((None, TOK_TILE, width), lambda b, j: (b, j, 0))


def _mod_spec():
    return pl.BlockSpec((None, None, 6, D_MODEL), lambda b, j: (b, j // LATENT_TILES, 0, 0))


def _const_spec(shape):
    return pl.BlockSpec(shape, lambda b, j: (0,) * len(shape))


def _table_spec():
    return pl.BlockSpec((TOK_TILE, MXU_W), lambda b, j: (j, 0))


def _qkv_kernel(h_ref, mod_ref, g_ref, w_ref, cos_ref, up_ref, dn_ref, q_ref, k_ref, v_ref,
                *, nq, nk, qscale):
    u = _modulated(h_ref, mod_ref, g_ref, 0, 0).astype(_BF16)
    cos, up, dn = cos_ref[...], up_ref[...], dn_ref[...]
    quarter = A_HEAD_DIM // 4
    for blk in range(nq // MXU_W):
        x = _dot(u, w_ref[:, blk * MXU_W:(blk + 1) * MXU_W])
        x = _rope_block(x, cos, up, dn, quarter) * qscale
        q_ref[:, blk * MXU_W:(blk + 1) * MXU_W] = x.astype(_BF16)
    for blk in range(nk // MXU_W):
        x = _dot(u, w_ref[:, nq + blk * MXU_W:nq + (blk + 1) * MXU_W])
        k_ref[:, blk * MXU_W:(blk + 1) * MXU_W] = _rope_block(x, cos, up, dn, quarter).astype(_BF16)
    v_ref[...] = _dot(u, w_ref[:, nq + nk:]).astype(_BF16)


def _qkv_proj(h, mod_all, norms, w, tables, nq, nk, nv, qscale):
    batch = h.shape[0]
    return pl.pallas_call(
        functools.partial(_qkv_kernel, nq=nq, nk=nk, qscale=qscale),
        grid=(batch, TILES_PER_SAMPLE),
        in_specs=[_tok_spec(D_MODEL), _mod_spec(), _const_spec((4, D_MODEL)),
                  _const_spec((D_MODEL, nq + nk + nv)),
                  _table_spec(), _table_spec(), _table_spec()],
        out_specs=[_tok_spec(nq), _tok_spec(nk), _tok_spec(nv)],
        out_shape=[jax.ShapeDtypeStruct((batch, S_ALL, n), _BF16) for n in (nq, nk, nv)],
        compiler_params=_params(2),
        name="qkv_proj",
    )(h, mod_all, norms, w, *tables)


C_QK_PAD = LANES
C_IN_PAD = C_Q_LORA + C_KV_LORA + LANES


def _mla_proj_kernel(h_ref, mod_ref, g_ref, w_in_ref, qn_ref, kvn_ref, w_uq_ref, w_uk_ref, w_uv_ref,
                     cos_ref, up_ref, dn_ref, q_ref, k_ref, v_ref, *, qscale):
    u = _modulated(h_ref, mod_ref, g_ref, 0, 0).astype(_BF16)
    cos, up, dn = cos_ref[...], up_ref[...], dn_ref[...]
    quarter = C_ROPE // 4
    t = _dot(u, w_in_ref[...])
    cq = (_rms(t[:, :C_Q_LORA]) * qn_ref[...]).astype(_BF16)
    ckv = (_rms(t[:, C_Q_LORA:C_Q_LORA + C_KV_LORA]) * kvn_ref[...]).astype(_BF16)
    kr = _rope_block(t[:, C_Q_LORA + C_KV_LORA:], cos[:, :LANES], up[:, :LANES], dn[:, :LANES], quarter)
    kr = jnp.concatenate([kr] * (MXU_W // C_QK_PAD), axis=1)
    for blk in range(C_HEADS * C_QK_PAD // MXU_W):
        sl = slice(blk * MXU_W, (blk + 1) * MXU_W)
        q = _rope_block(_dot(cq, w_uq_ref[:, sl]), cos, up, dn, quarter) * qscale
        q_ref[:, sl] = q.astype(_BF16)
        k_ref[:, sl] = (_dot(ckv, w_uk_ref[:, sl]) + kr).astype(_BF16)
    v_ref[...] = _dot(ckv, w_uv_ref[...]).astype(_BF16)


def _mla_proj(h, mod_all, norms, w_in, q_norm, kv_norm, w_uq, w_uk, w_uv, tables, qscale):
    batch = h.shape[0]
    nqk = C_HEADS * C_QK_PAD
    nv = C_HEADS * C_V
    return pl.pallas_call(
        functools.partial(_mla_proj_kernel, qscale=qscale),
        grid=(batch, TILES_PER_SAMPLE),
        in_specs=[_tok_spec(D_MODEL), _mod_spec(), _const_spec((4, D_MODEL)),
                  _const_spec((D_MODEL, C_IN_PAD)), _const_spec((1, C_Q_LORA)),
                  _const_spec((1, C_KV_LORA)), _const_spec((C_Q_LORA, nqk)),
                  _const_spec((C_KV_LORA, nqk)), _const_spec((C_KV_LORA, nv)),
                  _table_spec(), _table_spec(), _table_spec()],
        out_specs=[_tok_spec(nqk), _tok_spec(nqk), _tok_spec(nv)],
        out_shape=[jax.ShapeDtypeStruct((batch, S_ALL, n), _BF16) for n in (nqk, nqk, nv)],
        compiler_params=_params(2),
        name="mla_proj",
    )(h, mod_all, norms, w_in, q_norm, kv_norm, w_uq, w_uk, w_uv, *tables)


A_BLOCK = A_KV_HEADS * A_HEAD_DIM
A_KEYS = 2 * TOK_TILE + CTX_LEN


def _window_kernel(sink_ref, q_ref, kp_ref, kc_ref, kn_ref, kx_ref, vp_ref, vc_ref, vn_ref, vx_ref,
                   o_ref):
    j = pl.program_id(1)
    k_all = jnp.concatenate([kp_ref[...], kc_ref[...], kn_ref[...], kx_ref[...]], axis=0)
    v_all = jnp.concatenate([vp_ref[...], vc_ref[...], vn_ref[...], vx_ref[...]], axis=0)
    qi = lax.broadcasted_iota(jnp.int32, (TOK_TILE, A_KEYS), 0)
    c = lax.broadcasted_iota(jnp.int32, (TOK_TILE, A_KEYS), 1)
    kpos = TOK_TILE * j - WINDOW + c
    local = ((c - qi >= 0) & (c - qi <= 2 * WINDOW) & (kpos >= 0) & (kpos < SEQ)
             & (j < LATENT_TILES))
    mask = jnp.concatenate([local | (c >= 2 * TOK_TILE)] * A_GROUP, axis=0)
    lane_group = lax.broadcasted_iota(jnp.int32, (TOK_TILE, A_BLOCK), 1) // A_HEAD_DIM
    out = [jnp.zeros((TOK_TILE, A_BLOCK), _F32) for _ in range(A_GROUP)]
    for g in range(A_KV_HEADS):
        sel = lane_group == g
        qg = jnp.concatenate(
            [jnp.where(sel, q_ref[:, r * A_BLOCK:(r + 1) * A_BLOCK], jnp.zeros((), _BF16))
             for r in range(A_GROUP)], axis=0)
        s = jnp.where(mask, _dot_nt(qg, k_all), NEG_INF)
        sink = jnp.concatenate(
            [jnp.full((TOK_TILE, 1), sink_ref[g * A_GROUP + r], _F32) for r in range(A_GROUP)], axis=0)
        m = jnp.maximum(jnp.max(s, axis=-1, keepdims=True), sink)
        p = jnp.exp(s - m)
        denom = jnp.sum(p, axis=-1, keepdims=True) + jnp.exp(sink - m)
        og = _dot(p.astype(_BF16), v_all) / denom
        for r in range(A_GROUP):
            out[r] = jnp.where(sel, og[r * TOK_TILE:(r + 1) * TOK_TILE], out[r])
    for r in range(A_GROUP):
        o_ref[:, r * A_BLOCK:(r + 1) * A_BLOCK] = out[r].astype(_BF16)


def _window_attention(q, k, v, sink, n_tiles):
    batch = q.shape[0]
    half = TOK_TILE // 2
    last_half = S_ALL // half - 1

    def half_spec(fn):
        return pl.BlockSpec((None, half, A_BLOCK), fn)

    prev_spec = half_spec(lambda b, j: (b, jnp.maximum(2 * j - 1, 0), 0))
    next_spec = half_spec(lambda b, j: (b, jnp.minimum(2 * j + 2, last_half), 0))
    cur_spec = pl.BlockSpec((None, TOK_TILE, A_BLOCK), lambda b, j: (b, j, 0))
    ctx_spec = pl.BlockSpec((None, CTX_LEN, A_BLOCK), lambda b, j: (b, LATENT_TILES, 0))
    kv_specs = [prev_spec, cur_spec, next_spec, ctx_spec]
    return pl.pallas_call(
        _window_kernel,
        grid=(batch, n_tiles),
        in_specs=[pl.BlockSpec(memory_space=pltpu.SMEM), _tok_spec(D_MODEL)] + kv_specs + kv_specs,
        out_specs=_tok_spec(D_MODEL),
        out_shape=jax.ShapeDtypeStruct((batch, n_tiles * TOK_TILE, D_MODEL), _BF16),
        compiler_params=_params(2),
        name="window_attention",
    )(sink, q, k, k, k, k, v, v, v, v)


def _attend(q, k_ref, v_ref, scratch, start, size):
    s_ref, m_ref, p_ref, l_ref = scratch
    rows = q.shape[0]
    half = rows // 2
    s_ref[:half, :size] = _dot_nt(q[:half], k_ref[start:start + size, :])
    s_ref[half:, :size] = _dot_nt(q[half:], k_ref[start:start + size, :])

    def row_max(i, carry):
        rs = pl.ds(pl.multiple_of(i * SOFTMAX_ROWS, SOFTMAX_ROWS), SOFTMAX_ROWS)
        parts = [s_ref[rs, c * LANES:(c + 1) * LANES] for c in range(PARTIALS)]
        for c in range(PARTIALS, size // LANES):
            parts[c % PARTIALS] = jnp.maximum(parts[c % PARTIALS], s_ref[rs, c * LANES:(c + 1) * LANES])
        m_ref[rs, :] = functools.reduce(jnp.maximum, parts)
        return carry

    def row_exp(i, carry):
        rs = pl.ds(pl.multiple_of(i * SOFTMAX_ROWS, SOFTMAX_ROWS), SOFTMAX_ROWS)
        m = m_ref[rs, :]
        parts = [None] * PARTIALS
        for c in range(size // LANES):
            p = jnp.exp2(s_ref[rs, c * LANES:(c + 1) * LANES] - m)
            parts[c % PARTIALS] = p if parts[c % PARTIALS] is None else parts[c % PARTIALS] + p
            p_ref[rs, c * LANES:(c + 1) * LANES] = p.astype(_BF16)
        l_ref[rs, :] = functools.reduce(jnp.add, [x for x in parts if x is not None])
        return carry

    lax.fori_loop(0, rows // SOFTMAX_ROWS, row_max, 0, unroll=SOFTMAX_UNROLL)
    m_ref[...] = jnp.broadcast_to(jnp.max(m_ref[...], axis=-1, keepdims=True), m_ref.shape)
    lax.fori_loop(0, rows // SOFTMAX_ROWS, row_exp, 0, unroll=SOFTMAX_UNROLL)
    o_lo = _dot(p_ref[:half, :size], v_ref[start:start + size, :])
    o_hi = _dot(p_ref[half:, :size], v_ref[start:start + size, :])
    return jnp.concatenate([o_lo, o_hi], axis=0) / jnp.sum(l_ref[...], axis=-1, keepdims=True)


def _attend_tile(q, k_ref, v_ref, scratch, emit):
    j = pl.program_id(2)

    @pl.when(j < LATENT_TILES)
    def _():
        emit(_attend(q, k_ref, v_ref, scratch, 0, S_ALL))

    @pl.when(j == LATENT_TILES)
    def _():
        emit(_attend(q, k_ref, v_ref, scratch, SEQ, CTX_LEN))


def _attend_scratch(rows):
    return [pltpu.VMEM((rows, S_ALL), _F32), pltpu.VMEM((rows, LANES), _F32),
            pltpu.VMEM((rows, S_ALL), _BF16), pltpu.VMEM((rows, LANES), _F32)]


def _diff_kernel(lam_ref, subln_ref, q_ref, k_ref, v_ref, o_ref, *scratch, lam_init):
    lam = lam_ref[...]
    lam_full = (jnp.exp(jnp.sum(lam[0:1] * lam[1:2], axis=-1, keepdims=True))
                - jnp.exp(jnp.sum(lam[2:3] * lam[3:4], axis=-1, keepdims=True)) + lam_init)
    q = q_ref[...]
    first = lax.broadcasted_iota(jnp.int32, q.shape, 1) < B_HEAD_DIM
    zero = jnp.zeros((), _BF16)
    q2 = jnp.concatenate([jnp.where(first, q, zero), jnp.where(first, zero, q)], axis=0)

    def emit(a):
        o = a[:TOK_TILE] - lam_full * a[TOK_TILE:]
        o = _rms(o) * subln_ref[...] * (1.0 - lam_init)
        o_ref[...] = o.astype(_BF16)

    _attend_tile(q2, k_ref, v_ref, scratch, emit)


def _diff_attention(q, k, v, lam, subln, lam_init, n_tiles):
    batch = q.shape[0]
    head_w = 2 * B_HEAD_DIM
    q_spec = pl.BlockSpec((None, TOK_TILE, head_w), lambda b, h, j: (b, j, h))
    kv_spec = pl.BlockSpec((None, S_ALL, head_w), lambda b, h, j: (b, 0, h))
    rows = 2 * TOK_TILE
    return pl.pallas_call(
        functools.partial(_diff_kernel, lam_init=lam_init),
        grid=(batch, B_HEADS, n_tiles),
        in_specs=[pl.BlockSpec((4, B_HEAD_DIM), lambda b, h, j: (0, 0)),
                  pl.BlockSpec((1, head_w), lambda b, h, j: (0, 0)),
                  q_spec, kv_spec, kv_spec],
        out_specs=q_spec,
        out_shape=jax.ShapeDtypeStruct((batch, n_tiles * TOK_TILE, B_HEADS * head_w), _BF16),
        scratch_shapes=_attend_scratch(rows),
        compiler_params=_params(3),
        name="diff_attention",
    )(lam, subln, q, k, v)


def _mla_kernel(q_ref, k_ref, v_ref, o_ref, *scratch):
    q = q_ref[...]
    first = lax.broadcasted_iota(jnp.int32, q.shape, 1) < C_QK_PAD
    zero = jnp.zeros((), _BF16)
    q2 = jnp.concatenate([jnp.where(first, q, zero), jnp.where(first, zero, q)], axis=0)
    first_v = lax.broadcasted_iota(jnp.int32, (TOK_TILE, 2 * C_V), 1) < C_V

    def emit(a):
        o_ref[...] = jnp.where(first_v, a[:TOK_TILE], a[TOK_TILE:]).astype(_BF16)

    _attend_tile(q2, k_ref, v_ref, scratch, emit)


def _mla_attention(q, k, v, n_tiles):
    batch = q.shape[0]
    pairs = C_HEADS // 2
    q_spec = pl.BlockSpec((None, TOK_TILE, 2 * C_QK_PAD), lambda b, h, j: (b, j, h))
    k_spec = pl.BlockSpec((None, S_ALL, 2 * C_QK_PAD), lambda b, h, j: (b, 0, h))
    v_spec = pl.BlockSpec((None, S_ALL, 2 * C_V), lambda b, h, j: (b, 0, h))
    o_spec = pl.BlockSpec((None, TOK_TILE, 2 * C_V), lambda b, h, j: (b, j, h))
    rows = 2 * TOK_TILE
    return pl.pallas_call(
        _mla_kernel,
        grid=(batch, pairs, n_tiles),
        in_specs=[q_spec, k_spec, v_spec],
        out_specs=o_spec,
        out_shape=jax.ShapeDtypeStruct((batch, n_tiles * TOK_TILE, C_HEADS * C_V), _BF16),
        scratch_shapes=_attend_scratch(rows),
        compiler_params=_params(3),
        name="mla_attention",
    )(q, k, v)


def _post_kernel(a_ref, h_ref, mod_ref, g_ref, wo_ref, w1_ref, w2_ref, o_ref):
    y = _dot(a_ref[...], wo_ref[...])
    h1 = h_ref[...] + mod_ref[2:3, :] * (_rms(y) * g_ref[1:2, :])
    u = (_rms(h1) * g_ref[2:3, :]) * (1.0 + mod_ref[4:5, :]) + mod_ref[3:4, :]
    t = jnp.maximum(_dot(u.astype(_BF16), w1_ref[...]), 0.0)
    z = _dot((t * t).astype(_BF16), w2_ref[...])
    o_ref[...] = h1 + mod_ref[5:6, :] * (_rms(z) * g_ref[3:4, :])


def _post(a, h, mod_all, norms, w_o, w1, w2, n_tiles):
    batch = h.shape[0]
    return pl.pallas_call(
        _post_kernel,
        grid=(batch, n_tiles),
        in_specs=[_tok_spec(D_MODEL), _tok_spec(D_MODEL), _mod_spec(), _const_spec((4, D_MODEL)),
                  _const_spec((D_MODEL, D_MODEL)), _const_spec((D_MODEL, D_FF)),
                  _const_spec((D_FF, D_MODEL))],
        out_specs=_tok_spec(D_MODEL),
        out_shape=jax.ShapeDtypeStruct((batch, n_tiles * TOK_TILE, D_MODEL), _F32),
        compiler_params=_params(2),
        name="post",
    )(a, h, mod_all, norms, w_o, w1, w2)


def _a_head_perm():
    new = np.arange(A_HEADS * A_HEAD_DIM)
    r, g, d = new // A_BLOCK, (new % A_BLOCK) // A_HEAD_DIM, new % A_HEAD_DIM
    return (g * A_GROUP + r) * A_HEAD_DIM + d


def _layer_window(h, mod_all, norms, w_qkv, sink, w_o, n_tiles, tables):
    nq = A_HEADS * A_HEAD_DIM
    nk = A_KV_HEADS * A_HEAD_DIM
    perm = _a_head_perm()
    w = jnp.concatenate([w_qkv[:, :nq][:, perm], w_qkv[:, nq:]], axis=1).astype(_BF16)
    q, k, v = _qkv_proj(h, mod_all, norms, w, tables, nq, nk, nk, A_HEAD_DIM ** -0.5)
    a = _window_attention(q, k, v, sink, n_tiles)
    return a, w_o[perm, :]


def _layer_diff(h, mod_all, norms, w_qkv, lam, subln, w_o, layer_idx, n_tiles, tables):
    n = B_HEADS * 2 * B_HEAD_DIM
    lam_init = 0.8 - 0.6 * math.exp(-0.3 * layer_idx)
    q, k, v = _qkv_proj(h, mod_all, norms, w_qkv.astype(_BF16), tables, n, n, n,
                        B_HEAD_DIM ** -0.5 * LOG2E)
    a = _diff_attention(q, k, v, lam, subln.reshape(1, -1), lam_init, n_tiles)
    return a, w_o


def _layer_mla(h, mod_all, norms, w_in, q_norm, kv_norm, w_uq, w_ukv, w_o, n_tiles, tables):
    dqk = C_NOPE + C_ROPE
    zeros = functools.partial(jnp.zeros, dtype=_F32)
    w_in_p = jnp.concatenate(
        [w_in[:, :C_Q_LORA + C_KV_LORA], zeros((D_MODEL, C_NOPE)), w_in[:, C_Q_LORA + C_KV_LORA:],
         zeros((D_MODEL, C_QK_PAD - dqk))], axis=1).astype(_BF16)
    w_uq_p = jnp.pad(w_uq.reshape(C_Q_LORA, C_HEADS, dqk), ((0, 0), (0, 0), (0, C_QK_PAD - dqk)))
    w_uq_p = w_uq_p.reshape(C_Q_LORA, C_HEADS * C_QK_PAD).astype(_BF16)
    w_ukv_h = w_ukv.reshape(C_KV_LORA, C_HEADS, C_NOPE + C_V)
    w_uk_p = jnp.pad(w_ukv_h[:, :, :C_NOPE], ((0, 0), (0, 0), (0, C_QK_PAD - C_NOPE)))
    w_uk_p = w_uk_p.reshape(C_KV_LORA, C_HEADS * C_QK_PAD).astype(_BF16)
    w_uv = w_ukv_h[:, :, C_NOPE:].reshape(C_KV_LORA, C_HEADS * C_V).astype(_BF16)
    q, k, v = _mla_proj(h, mod_all, norms, w_in_p, q_norm.reshape(1, -1), kv_norm.reshape(1, -1),
                        w_uq_p, w_uk_p, w_uv, tables, dqk ** -0.5 * LOG2E)
    a = _mla_attention(q, k, v, n_tiles)
    return a, w_o


def kernel(x, c, ctx, c_ctx, l0_ada_w, l0_ada_b, l0_norms, l0_w_qkv, l0_sink, l0_w_o, l0_mlp_w1, l0_mlp_w2, l1_ada_w, l1_ada_b, l1_norms, l1_w_qkv, l1_lambda, l1_subln, l1_w_o, l1_mlp_w1, l1_mlp_w2, l2_ada_w, l2_ada_b, l2_norms, l2_w_in, l2_q_norm, l2_kv_norm, l2_w_uq, l2_w_ukv, l2_w_o, l2_mlp_w1, l2_mlp_w2, l3_ada_w, l3_ada_b, l3_norms, l3_w_qkv, l3_sink, l3_w_o, l3_mlp_w1, l3_mlp_w2):
    batch = x.shape[0]
    assert x.shape == (batch, SEQ, D_MODEL) and ctx.shape == (batch, CTX_LEN, D_MODEL) and batch < 16
    layers = [
        (l0_ada_w, l0_ada_b, l0_norms, l0_w_o, l0_mlp_w1, l0_mlp_w2),
        (l1_ada_w, l1_ada_b, l1_norms, l1_w_o, l1_mlp_w1, l1_mlp_w2),
        (l2_ada_w, l2_ada_b, l2_norms, l2_w_o, l2_mlp_w1, l2_mlp_w2),
        (l3_ada_w, l3_ada_b, l3_norms, l3_w_o, l3_mlp_w1, l3_mlp_w2),
    ]
    tables_ab = _rope_tables(A_HEAD_DIM, 0, A_HEAD_DIM)
    tables_c = _rope_tables(C_ROPE, C_NOPE, LANES)
    cc = jnp.zeros((16, D_MODEL), _F32).at[:batch].set(c).at[batch].set(c_ctx)
    h = jnp.concatenate([x, ctx], axis=1)
    depth = len(layers)
    for i, (ada_w, ada_b, norms, w_o, w1, w2) in enumerate(layers):
        n_tiles = LATENT_TILES if i == depth - 1 else TILES_PER_SAMPLE
        mod = _ada(cc, ada_w, ada_b)
        mod_lat = mod[:batch].reshape(batch, 1, 6, D_MODEL)
        mod_ctx = jnp.broadcast_to(mod[batch].reshape(1, 1, 6, D_MODEL), (batch, 1, 6, D_MODEL))
        mod_all = jnp.concatenate([mod_lat, mod_ctx], axis=1)
        if i == 0:
            a, w_o = _layer_window(h, mod_all, norms, l0_w_qkv, l0_sink, w_o, n_tiles, tables_ab)
        elif i == 1:
            a, w_o = _layer_diff(h, mod_all, norms, l1_w_qkv, l1_lambda, l1_subln, w_o, i, n_tiles,
                                 tables_ab)
        elif i == 2:
            a, w_o = _layer_mla(h, mod_all, norms, l2_w_in, l2_q_norm, l2_kv_norm, l2_w_uq, l2_w_ukv,
                                w_o, n_tiles, tables_c)
        else:
            a, w_o = _layer_window(h, mod_all, norms, l3_w_qkv, l3_sink, w_o, n_tiles, tables_ab)
        h = _post(a, h, mod_all, norms, w_o.astype(_BF16), w1.astype(_BF16), w2.astype(_BF16), n_tiles)
    return h
```

```python
import functools
import math
from typing import Any, NamedTuple

import jax
import jax.numpy as jnp
import numpy as np
from jax import lax
from jax.experimental import pallas as pl
from jax.experimental.pallas import tpu as pltpu

D_MODEL = 1024
SEQ = 4096
CTX_LEN = 256
S_ALL = SEQ + CTX_LEN
GRID_W = 64
D_FF = 4 * D_MODEL
EPS = 1e-6
ROPE_THETA = 10000.0
NEG_INF = -1e30
WINDOW = 128
LOG2E = math.log2(math.e)

A_HEADS = 16
A_KV_HEADS = 4
A_GROUP = 4
A_HEAD_DIM = 64
B_HEADS = 8
B_HEAD_DIM = 64
C_HEADS = 16
C_Q_LORA = 384
C_KV_LORA = 256
C_NOPE = 64
C_ROPE = 32
C_V = 64

LANES = 128
MXU_W = 256
TOK_TILE = 256
TILES_PER_SAMPLE = S_ALL // TOK_TILE
LATENT_TILES = SEQ // TOK_TILE
PARTIALS = 2
VMEM_LIMIT = 56 * 1024 * 1024

_BF16 = jnp.bfloat16
_F32 = jnp.float32


def _params(n_parallel):
    return pltpu.CompilerParams(
        dimension_semantics=("parallel",) * n_parallel, vmem_limit_bytes=VMEM_LIMIT)


def _rms(x):
    return x * lax.rsqrt(jnp.mean(x * x, axis=-1, keepdims=True) + EPS)


def _dot(a, b):
    return jnp.dot(a, b, preferred_element_type=_F32)


def _dot_nt(a, b):
    return lax.dot_general(a, b, (((1,), (1,)), ((), ())), preferred_element_type=_F32)


def _rope_tables(d_rot, lane_off, period):
    q = d_rot // 4
    da = d_rot // 2
    inv = ROPE_THETA ** (-jnp.arange(0, da, 2, dtype=_F32) / da)
    t = jnp.arange(SEQ, dtype=jnp.int32)
    rows = (t // GRID_W).astype(_F32)[:, None] * inv[None, :]
    cols = (t % GRID_W).astype(_F32)[:, None] * inv[None, :]
    zero = jnp.zeros((SEQ, q), _F32)
    cos = jnp.concatenate([jnp.cos(rows), jnp.cos(rows), jnp.cos(cols), jnp.cos(cols)], axis=1)
    sin_up = jnp.concatenate([-jnp.sin(rows), zero, -jnp.sin(cols), zero], axis=1)
    sin_dn = jnp.concatenate([zero, jnp.sin(rows), zero, jnp.sin(cols)], axis=1)

    def place(tab, fill):
        unit = jnp.full((SEQ, period), fill, _F32).at[:, lane_off:lane_off + d_rot].set(tab)
        full = jnp.tile(unit, (1, MXU_W // period))
        ctx = jnp.full((CTX_LEN, MXU_W), fill, _F32)
        return jnp.concatenate([full, ctx], axis=0)

    return place(cos, 1.0), place(sin_up, 0.0), place(sin_dn, 0.0)


def _rope_block(x, cos, sin_up, sin_dn, q):
    width = x.shape[1]
    up = pltpu.roll(x, width - q, 1)
    dn = pltpu.roll(x, q, 1)
    return x * cos + up * sin_up + dn * sin_dn


def _ada_kernel(c_ref, w_ref, b_ref, o_ref):
    c = c_ref[...]
    s = c * (1.0 / (1.0 + jnp.exp(-c)))
    o_ref[...] = jnp.dot(s, w_ref[...], preferred_element_type=_F32,
                         precision=lax.Precision.HIGHEST) + b_ref[...]


def _ada(cc, w, b):
    n = w.shape[1]
    bn = D_MODEL
    return pl.pallas_call(
        _ada_kernel,
        grid=(n // bn,),
        in_specs=[pl.BlockSpec((16, D_MODEL), lambda j: (0, 0)),
                  pl.BlockSpec((D_MODEL, bn), lambda j: (0, j)),
                  pl.BlockSpec((1, bn), lambda j: (0, j))],
        out_specs=pl.BlockSpec((16, bn), lambda j: (0, j)),
        out_shape=jax.ShapeDtypeStruct((16, n), _F32),
        compiler_params=_params(1),
        name="ada",
    )(cc, w, b.reshape(1, n))


def _modulated(h_ref, mod_ref, g_ref, norm_row, shift_row):
    y = _rms(h_ref[...]) * g_ref[norm_row:norm_row + 1, :]
    return y * (1.0 + mod_ref[shift_row + 1:shift_row + 2, :]) + mod_ref[shift_row:shift_row + 1, :]


def _tok_spec(width):
    return pl.BlockSpec((None, TOK_TILE, width), lambda b, j: (b, j, 0))


def _mod_spec():
    return pl.BlockSpec((None, None, 6, D_MODEL), lambda b, j: (b, j // LATENT_TILES, 0, 0))


def _const_spec(shape):
    return pl.BlockSpec(shape, lambda b, j: (0,) * len(shape))


def _table_spec():
    return pl.BlockSpec((TOK_TILE, MXU_W), lambda b, j: (j, 0))


def _qkv_kernel(h_ref, mod_ref, g_ref, w_ref, cos_ref, up_ref, dn_ref, q_ref, k_ref, v_ref,
                *, nq, nk, qscale):
    u = _modulated(h_ref, mod_ref, g_ref, 0, 0).astype(_BF16)
    cos, up, dn = cos_ref[...], up_ref[...], dn_ref[...]
    quarter = A_HEAD_DIM // 4
    for blk in range(nq // MXU_W):
        x = _dot(u, w_ref[:, blk * MXU_W:(blk + 1) * MXU_W])
        x = _rope_block(x, cos, up, dn, quarter) * qscale
        q_ref[:, blk * MXU_W:(blk + 1) * MXU_W] = x.astype(_BF16)
    for blk in range(nk // MXU_W):
        x = _dot(u, w_ref[:, nq + blk * MXU_W:nq + (blk + 1) * MXU_W])
        k_ref[:, blk * MXU_W:(blk + 1) * MXU_W] = _rope_block(x, cos, up, dn, quarter).astype(_BF16)
    v_ref[...] = _dot(u, w_ref[:, nq + nk:]).astype(_BF16)


def _qkv_proj(h, mod_all, norms, w, tables, nq, nk, nv, qscale):
    batch = h.shape[0]
    return pl.pallas_call(
        functools.partial(_qkv_kernel, nq=nq, nk=nk, qscale=qscale),
        grid=(batch, TILES_PER_SAMPLE),
        in_specs=[_tok_spec(D_MODEL), _mod_spec(), _const_spec((4, D_MODEL)),
                  _const_spec((D_MODEL, nq + nk + nv)),
                  _table_spec(), _table_spec(), _table_spec()],
        out_specs=[_tok_spec(nq), _tok_spec(nk), _tok_spec(nv)],
        out_shape=[jax.ShapeDtypeStruct((batch, S_ALL, n), _BF16) for n in (nq, nk, nv)],
        compiler_params=_params(2),
        name="qkv_proj",
    )(h, mod_all, norms, w, *tables)


C_QK_PAD = LANES
C_IN_PAD = C_Q_LORA + C_KV_LORA + LANES


def _mla_proj_kernel(h_ref, mod_ref, g_ref, w_in_ref, qn_ref, kvn_ref, w_uq_ref, w_uk_ref, w_uv_ref,
                     cos_ref, up_ref, dn_ref, q_ref, k_ref, v_ref, *, qscale):
    u = _modulated(h_ref, mod_ref, g_ref, 0, 0).astype(_BF16)
    cos, up, dn = cos_ref[...], up_ref[...], dn_ref[...]
    quarter = C_ROPE // 4
    t = _dot(u, w_in_ref[...])
    cq = (_rms(t[:, :C_Q_LORA]) * qn_ref[...]).astype(_BF16)
    ckv = (_rms(t[:, C_Q_LORA:C_Q_LORA + C_KV_LORA]) * kvn_ref[...]).astype(_BF16)
    kr = _rope_block(t[:, C_Q_LORA + C_KV_LORA:], cos[:, :LANES], up[:, :LANES], dn[:, :LANES], quarter)
    kr = jnp.concatenate([kr] * (MXU_W // C_QK_PAD), axis=1)
    for blk in range(C_HEADS * C_QK_PAD // MXU_W):
        sl = slice(blk * MXU_W, (blk + 1) * MXU_W)
        q = _rope_block(_dot(cq, w_uq_ref[:, sl]), cos, up, dn, quarter) * qscale
        q_ref[:, sl] = q.astype(_BF16)
        k_ref[:, sl] = (_dot(ckv, w_uk_ref[:, sl]) + kr).astype(_BF16)
    v_ref[...] = _dot(ckv, w_uv_ref[...]).astype(_BF16)


def _mla_proj(h, mod_all, norms, w_in, q_norm, kv_norm, w_uq, w_uk, w_uv, tables, qscale):
    batch = h.shape[0]
    nqk = C_HEADS * C_QK_PAD
    nv = C_HEADS * C_V
    return pl.pallas_call(
        functools.partial(_mla_proj_kernel, qscale=qscale),
        grid=(batch, TILES_PER_SAMPLE),
        in_specs=[_tok_spec(D_MODEL), _mod_spec(), _const_spec((4, D_MODEL)),
                  _const_spec((D_MODEL, C_IN_PAD)), _const_spec((1, C_Q_LORA)),
                  _const_spec((1, C_KV_LORA)), _const_spec((C_Q_LORA, nqk)),
                  _const_spec((C_KV_LORA, nqk)), _const_spec((C_KV_LORA, nv)),
                  _table_spec(), _table_spec(), _table_spec()],
        out_specs=[_tok_spec(nqk), _tok_spec(nqk), _tok_spec(nv)],
        out_shape=[jax.ShapeDtypeStruct((batch, S_ALL, n), _BF16) for n in (nqk, nqk, nv)],
        compiler_params=_params(2),
        name="mla_proj",
    )(h, mod_all, norms, w_in, q_norm, kv_norm, w_uq, w_uk, w_uv, *tables)


A_BLOCK = A_KV_HEADS * A_HEAD_DIM
A_KEYS = 2 * TOK_TILE + CTX_LEN


def _window_kernel(sink_ref, q_ref, kp_ref, kc_ref, kn_ref, kx_ref, vp_ref, vc_ref, vn_ref, vx_ref,
                   o_ref):
    j = pl.program_id(1)
    k_all = jnp.concatenate([kp_ref[...], kc_ref[...], kn_ref[...], kx_ref[...]], axis=0)
    v_all = jnp.concatenate([vp_ref[...], vc_ref[...], vn_ref[...], vx_ref[...]], axis=0)
    qi = lax.broadcasted_iota(jnp.int32, (TOK_TILE, A_KEYS), 0)
    c = lax.broadcasted_iota(jnp.int32, (TOK_TILE, A_KEYS), 1)
    kpos = TOK_TILE * j - WINDOW + c
    local = ((c - qi >= 0) & (c - qi <= 2 * WINDOW) & (kpos >= 0) & (kpos < SEQ)
             & (j < LATENT_TILES))
    mask = jnp.concatenate([local | (c >= 2 * TOK_TILE)] * A_GROUP, axis=0)
    lane_group = lax.broadcasted_iota(jnp.int32, (TOK_TILE, A_BLOCK), 1) // A_HEAD_DIM
    out = [jnp.zeros((TOK_TILE, A_BLOCK), _F32) for _ in range(A_GROUP)]
    for g in range(A_KV_HEADS):
        sel = lane_group == g
        qg = jnp.concatenate(
            [jnp.where(sel, q_ref[:, r * A_BLOCK:(r + 1) * A_BLOCK], jnp.zeros((), _BF16))
             for r in range(A_GROUP)], axis=0)
        s = jnp.where(mask, _dot_nt(qg, k_all), NEG_INF)
        sink = jnp.concatenate(
            [jnp.full((TOK_TILE, 1), sink_ref[g * A_GROUP + r], _F32) for r in range(A_GROUP)], axis=0)
        m = jnp.maximum(jnp.max(s, axis=-1, keepdims=True), sink)
        p = jnp.exp(s - m)
        denom = jnp.sum(p, axis=-1, keepdims=True) + jnp.exp(sink - m)
        og = _dot(p.astype(_BF16), v_all) / denom
        for r in range(A_GROUP):
            out[r] = jnp.where(sel, og[r * TOK_TILE:(r + 1) * TOK_TILE], out[r])
    for r in range(A_GROUP):
        o_ref[:, r * A_BLOCK:(r + 1) * A_BLOCK] = out[r].astype(_BF16)


def _window_attention(q, k, v, sink, n_tiles):
    batch = q.shape[0]
    half = TOK_TILE // 2
    last_half = S_ALL // half - 1

    def half_spec(fn):
        return pl.BlockSpec((None, half, A_BLOCK), fn)

    prev_spec = half_spec(lambda b, j: (b, jnp.maximum(2 * j - 1, 0), 0))
    next_spec = half_spec(lambda b, j: (b, jnp.minimum(2 * j + 2, last_half), 0))
    cur_spec = pl.BlockSpec((None, TOK_TILE, A_BLOCK), lambda b, j: (b, j, 0))
    ctx_spec = pl.BlockSpec((None, CTX_LEN, A_BLOCK), lambda b, j: (b, LATENT_TILES, 0))
    kv_specs = [prev_spec, cur_spec, next_spec, ctx_spec]
    return pl.pallas_call(
        _window_kernel,
        grid=(batch, n_tiles),
        in_specs=[pl.BlockSpec(memory_space=pltpu.SMEM), _tok_spec(D_MODEL)] + kv_specs + kv_specs,
        out_specs=_tok_spec(D_MODEL),
        out_shape=jax.ShapeDtypeStruct((batch, n_tiles * TOK_TILE, D_MODEL), _BF16),
        compiler_params=_params(2),
        name="window_attention",
    )(sink, q, k, k, k, k, v, v, v, v)


STACK_ROWS = 2 * TOK_TILE
KEY_TILE = MXU_W
N_KEY_TILES = S_ALL // KEY_TILE
FUSED_TRIPS = N_KEY_TILES - 1
EXP_ROWS = STACK_ROWS // FUSED_TRIPS
FUSED_UNROLL = 4


def _stack_masked(q, split):
    first = lax.broadcasted_iota(jnp.int32, q.shape, 1) < split
    zero = jnp.zeros((), _BF16)
    return jnp.concatenate([jnp.where(first, q, zero), jnp.where(first, zero, q)], axis=0)


def _tile_rows(t):
    if isinstance(t, int):
        return pl.ds(t * KEY_TILE, KEY_TILE)
    return pl.ds(pl.multiple_of(t * KEY_TILE, KEY_TILE), KEY_TILE)


class _TileScratch(NamedTuple):
    q: Any
    s: Any
    m: Any
    p: Any
    l: Any
    acc: Any


def _qk_tile(tile, t, k_ref):
    keys = k_ref[_tile_rows(t), :]
    for half in range(2):
        rows = pl.ds(half * TOK_TILE, TOK_TILE)
        s = _dot_nt(tile.q[rows, :], keys)
        tile.s[t, rows, :] = s
        tile.m[rows, :] = jnp.maximum(tile.m[rows, :], jnp.maximum(s[:, :LANES], s[:, LANES:]))


def _reduce_max(tile):
    tile.m[...] = jnp.broadcast_to(jnp.max(tile.m[...], axis=-1, keepdims=True), tile.m.shape)


def _exp_rows(tile, i):
    rs = pl.ds(pl.multiple_of(i * EXP_ROWS, EXP_ROWS), EXP_ROWS)
    m = tile.m[rs, :]
    parts = [None] * PARTIALS
    n = 0
    for c in range(N_KEY_TILES):
        for lo in range(0, KEY_TILE, LANES):
            p = jnp.exp2(tile.s[c, rs, lo:lo + LANES] - m)
            parts[n % PARTIALS] = p if parts[n % PARTIALS] is None else parts[n % PARTIALS] + p
            tile.p[c, rs, lo:lo + LANES] = p.astype(_BF16)
            n += 1
    tile.l[rs, :] = functools.reduce(jnp.add, parts)


def _pv_chunk(tile, t, v_ref):
    vals = v_ref[_tile_rows(t), :]
    for half in range(2):
        rows = pl.ds(half * TOK_TILE, TOK_TILE)
        tile.acc[rows, :] += _dot(tile.p[t, rows, :], vals)


def _attend_pair(tiles, k_ref, v_ref):
    first, second = tiles
    for tile in tiles:
        tile.m[...] = jnp.full(tile.m.shape, -jnp.inf, _F32)
        tile.acc[...] = jnp.zeros(tile.acc.shape, _F32)
    for t in range(N_KEY_TILES):
        _qk_tile(first, t, k_ref)
    _reduce_max(first)

    def exp_first_scores_second(t, carry):
        _exp_rows(first, t)
        _qk_tile(second, t, k_ref)
        return carry

    lax.fori_loop(0, FUSED_TRIPS, exp_first_scores_second, 0, unroll=FUSED_UNROLL)
    _qk_tile(second, FUSED_TRIPS, k_ref)
    _reduce_max(second)

    def out_first_exp_second(t, carry):
        _pv_chunk(first, t, v_ref)
        _exp_rows(second, t)
        return carry

    lax.fori_loop(0, FUSED_TRIPS, out_first_exp_second, 0, unroll=FUSED_UNROLL)
    _pv_chunk(first, FUSED_TRIPS, v_ref)
    for t in range(N_KEY_TILES):
        _pv_chunk(second, t, v_ref)
    return [tile.acc[...] / jnp.sum(tile.l[...], axis=-1, keepdims=True) for tile in tiles]


def _pair_scratch(qk_width, v_width):
    one = [pltpu.VMEM((STACK_ROWS, qk_width), _BF16),
           pltpu.VMEM((N_KEY_TILES, STACK_ROWS, KEY_TILE), _F32),
           pltpu.VMEM((STACK_ROWS, LANES), _F32),
           pltpu.VMEM((N_KEY_TILES, STACK_ROWS, KEY_TILE), _BF16),
           pltpu.VMEM((STACK_ROWS, LANES), _F32),
           pltpu.VMEM((STACK_ROWS, v_width), _F32)]
    return one + one


def _pair_tiles(scratch):
    n = len(_TileScratch._fields)
    return _TileScratch(*scratch[:n]), _TileScratch(*scratch[n:])


def _attend_context(q, k_ref, v_ref):
    s = _dot_nt(q, k_ref[...])
    p = jnp.exp2(s - jnp.max(s, axis=-1, keepdims=True))
    return _dot(p.astype(_BF16), v_ref[...]) / jnp.sum(p, axis=-1, keepdims=True)


def _dense_attention(latent_kernel, context_kernel, params, q, k, v, heads, qk_width, v_width,
                     with_context, name):
    batch = q.shape[0]
    pair = 2 * TOK_TILE
    n_rows = S_ALL if with_context else SEQ
    out_shape = jax.ShapeDtypeStruct((batch, n_rows, heads * v_width), _BF16)

    param_specs = [pl.BlockSpec(p.shape, lambda *_, nd=p.ndim: (0,) * nd) for p in params]

    out = pl.pallas_call(
        latent_kernel,
        grid=(batch, heads, SEQ // pair),
        in_specs=param_specs + [
            pl.BlockSpec((None, pair, qk_width), lambda b, h, j: (b, j, h)),
            pl.BlockSpec((None, S_ALL, qk_width), lambda b, h, j: (b, 0, h)),
            pl.BlockSpec((None, S_ALL, v_width), lambda b, h, j: (b, 0, h))],
        out_specs=pl.BlockSpec((None, pair, v_width), lambda b, h, j: (b, j, h)),
        out_shape=out_shape,
        scratch_shapes=_pair_scratch(qk_width, v_width),
        compiler_params=_params(3),
        name=name,
    )(*params, q, k, v)
    if not with_context:
        return out
    ctx_block = SEQ // CTX_LEN
    return pl.pallas_call(
        context_kernel,
        grid=(batch, heads),
        in_specs=param_specs + [
            pl.BlockSpec((None, CTX_LEN, qk_width), lambda b, h: (b, ctx_block, h)),
            pl.BlockSpec((None, CTX_LEN, qk_width), lambda b, h: (b, ctx_block, h)),
            pl.BlockSpec((None, CTX_LEN, v_width), lambda b, h: (b, ctx_block, h)),
            pl.BlockSpec(memory_space=pl.ANY)],
        out_specs=pl.BlockSpec((None, CTX_LEN, v_width), lambda b, h: (b, ctx_block, h)),
        out_shape=out_shape,
        input_output_aliases={len(params) + 3: 0},
        compiler_params=_params(2),
        name=name + "_context",
    )(*params, q, k, v, out)


def _diff_finish(a, lam_ref, subln_ref, lam_init):
    lam = lam_ref[...]
    lam_full = (jnp.exp(jnp.sum(lam[0:1] * lam[1:2], axis=-1, keepdims=True))
                - jnp.exp(jnp.sum(lam[2:3] * lam[3:4], axis=-1, keepdims=True)) + lam_init)
    o = a[:TOK_TILE] - lam_full * a[TOK_TILE:]
    return (_rms(o) * subln_ref[...] * (1.0 - lam_init)).astype(_BF16)


def _diff_latent_kernel(lam_ref, subln_ref, q_ref, k_ref, v_ref, o_ref, *scratch, lam_init):
    tiles = _pair_tiles(scratch)
    for x in range(2):
        tiles[x].q[...] = _stack_masked(q_ref[x * TOK_TILE:(x + 1) * TOK_TILE, :], B_HEAD_DIM)
    outs = _attend_pair(tiles, k_ref, v_ref)
    for x in range(2):
        o_ref[x * TOK_TILE:(x + 1) * TOK_TILE, :] = _diff_finish(outs[x], lam_ref, subln_ref, lam_init)


def _diff_context_kernel(lam_ref, subln_ref, q_ref, k_ref, v_ref, buf_ref, o_ref, *, lam_init):
    del buf_ref
    a = _attend_context(_stack_masked(q_ref[...], B_HEAD_DIM), k_ref, v_ref)
    o_ref[...] = _diff_finish(a, lam_ref, subln_ref, lam_init)


def _diff_attention(q, k, v, lam, subln, lam_init, with_context):
    head_w = 2 * B_HEAD_DIM
    return _dense_attention(functools.partial(_diff_latent_kernel, lam_init=lam_init),
                            functools.partial(_diff_context_kernel, lam_init=lam_init),
                            (lam, subln), q, k, v, B_HEADS, head_w, head_w, with_context,
                            "diff_attention")


def _mla_finish(a):
    first_v = lax.broadcasted_iota(jnp.int32, (TOK_TILE, 2 * C_V), 1) < C_V
    return jnp.where(first_v, a[:TOK_TILE], a[TOK_TILE:]).astype(_BF16)


def _mla_latent_kernel(q_ref, k_ref, v_ref, o_ref, *scratch):
    tiles = _pair_tiles(scratch)
    for x in range(2):
        tiles[x].q[...] = _stack_masked(q_ref[x * TOK_TILE:(x + 1) * TOK_TILE, :], C_QK_PAD)
    outs = _attend_pair(tiles, k_ref, v_ref)
    for x in range(2):
        o_ref[x * TOK_TILE:(x + 1) * TOK_TILE, :] = _mla_finish(outs[x])


def _mla_context_kernel(q_ref, k_ref, v_ref, buf_ref, o_ref):
    del buf_ref
    o_ref[...] = _mla_finish(_attend_context(_stack_masked(q_ref[...], C_QK_PAD), k_ref, v_ref))


def _mla_attention(q, k, v, with_context):
    return _dense_attention(_mla_latent_kernel, _mla_context_kernel, (), q, k, v, C_HEADS // 2,
                            2 * C_QK_PAD, 2 * C_V, with_context, "mla_attention")


def _post_kernel(a_ref, h_ref, mod_ref, g_ref, wo_ref, w1_ref, w2_ref, o_ref):
    y = _dot(a_ref[...], wo_ref[...])
    h1 = h_ref[...] + mod_ref[2:3, :] * (_rms(y) * g_ref[1:2, :])
    u = (_rms(h1) * g_ref[2:3, :]) * (1.0 + mod_ref[4:5, :]) + mod_ref[3:4, :]
    t = jnp.maximum(_dot(u.astype(_BF16), w1_ref[...]), 0.0)
    z = _dot((t * t).astype(_BF16), w2_ref[...])
    o_ref[...] = h1 + mod_ref[5:6, :] * (_rms(z) * g_ref[3:4, :])


def _post(a, h, mod_all, norms, w_o, w1, w2, n_tiles):
    batch = h.shape[0]
    return pl.pallas_call(
        _post_kernel,
        grid=(batch, n_tiles),
        in_specs=[_tok_spec(D_MODEL), _tok_spec(D_MODEL), _mod_spec(), _const_spec((4, D_MODEL)),
                  _const_spec((D_MODEL, D_MODEL)), _const_spec((D_MODEL, D_FF)),
                  _const_spec((D_FF, D_MODEL))],
        out_specs=_tok_spec(D_MODEL),
        out_shape=jax.ShapeDtypeStruct((batch, n_tiles * TOK_TILE, D_MODEL), _F32),
        compiler_params=_params(2),
        name="post",
    )(a, h, mod_all, norms, w_o, w1, w2)


def _a_head_perm():
    new = np.arange(A_HEADS * A_HEAD_DIM)
    r, g, d = new // A_BLOCK, (new % A_BLOCK) // A_HEAD_DIM, new % A_HEAD_DIM
    return (g * A_GROUP + r) * A_HEAD_DIM + d


def _layer_window(h, mod_all, norms, w_qkv, sink, w_o, n_tiles, tables):
    nq = A_HEADS * A_HEAD_DIM
    nk = A_KV_HEADS * A_HEAD_DIM
    perm = _a_head_perm()
    w = jnp.concatenate([w_qkv[:, :nq][:, perm], w_qkv[:, nq:]], axis=1).astype(_BF16)
    q, k, v = _qkv_proj(h, mod_all, norms, w, tables, nq, nk, nk, A_HEAD_DIM ** -0.5)
    a = _window_attention(q, k, v, sink, n_tiles)
    return a, w_o[perm, :]


def _layer_diff(h, mod_all, norms, w_qkv, lam, subln, w_o, layer_idx, n_tiles, tables):
    n = B_HEADS * 2 * B_HEAD_DIM
    lam_init = 0.8 - 0.6 * math.exp(-0.3 * layer_idx)
    q, k, v = _qkv_proj(h, mod_all, norms, w_qkv.astype(_BF16), tables, n, n, n,
                        B_HEAD_DIM ** -0.5 * LOG2E)
    a = _diff_attention(q, k, v, lam, subln.reshape(1, -1), lam_init, n_tiles == TILES_PER_SAMPLE)
    return a, w_o


def _layer_mla(h, mod_all, norms, w_in, q_norm, kv_norm, w_uq, w_ukv, w_o, n_tiles, tables):
    dqk = C_NOPE + C_ROPE
    zeros = functools.partial(jnp.zeros, dtype=_F32)
    w_in_p = jnp.concatenate(
        [w_in[:, :C_Q_LORA + C_KV_LORA], zeros((D_MODEL, C_NOPE)), w_in[:, C_Q_LORA + C_KV_LORA:],
         zeros((D_MODEL, C_QK_PAD - dqk))], axis=1).astype(_BF16)
    w_uq_p = jnp.pad(w_uq.reshape(C_Q_LORA, C_HEADS, dqk), ((0, 0), (0, 0), (0, C_QK_PAD - dqk)))
    w_uq_p = w_uq_p.reshape(C_Q_LORA, C_HEADS * C_QK_PAD).astype(_BF16)
    w_ukv_h = w_ukv.reshape(C_KV_LORA, C_HEADS, C_NOPE + C_V)
    w_uk_p = jnp.pad(w_ukv_h[:, :, :C_NOPE], ((0, 0), (0, 0), (0, C_QK_PAD - C_NOPE)))
    w_uk_p = w_uk_p.reshape(C_KV_LORA, C_HEADS * C_QK_PAD).astype(_BF16)
    w_uv = w_ukv_h[:, :, C_NOPE:].reshape(C_KV_LORA, C_HEADS * C_V).astype(_BF16)
    q, k, v = _mla_proj(h, mod_all, norms, w_in_p, q_norm.reshape(1, -1), kv_norm.reshape(1, -1),
                        w_uq_p, w_uk_p, w_uv, tables, dqk ** -0.5 * LOG2E)
    a = _mla_attention(q, k, v, n_tiles == TILES_PER_SAMPLE)
    return a, w_o


def kernel(x, c, ctx, c_ctx, l0_ada_w, l0_ada_b, l0_norms, l0_w_qkv, l0_sink, l0_w_o, l0_mlp_w1, l0_mlp_w2, l1_ada_w, l1_ada_b, l1_norms, l1_w_qkv, l1_lambda, l1_subln, l1_w_o, l1_mlp_w1, l1_mlp_w2, l2_ada_w, l2_ada_b, l2_norms, l2_w_in, l2_q_norm, l2_kv_norm, l2_w_uq, l2_w_ukv, l2_w_o, l2_mlp_w1, l2_mlp_w2, l3_ada_w, l3_ada_b, l3_norms, l3_w_qkv, l3_sink, l3_w_o, l3_mlp_w1, l3_mlp_w2):
    batch = x.shape[0]
    assert x.shape == (batch, SEQ, D_MODEL) and ctx.shape == (batch, CTX_LEN, D_MODEL) and batch < 16
    layers = [
        (l0_ada_w, l0_ada_b, l0_norms, l0_w_o, l0_mlp_w1, l0_mlp_w2),
        (l1_ada_w, l1_ada_b, l1_norms, l1_w_o, l1_mlp_w1, l1_mlp_w2),
        (l2_ada_w, l2_ada_b, l2_norms, l2_w_o, l2_mlp_w1, l2_mlp_w2),
        (l3_ada_w, l3_ada_b, l3_norms, l3_w_o, l3_mlp_w1, l3_mlp_w2),
    ]
    tables_ab = _rope_tables(A_HEAD_DIM, 0, A_HEAD_DIM)
    tables_c = _rope_tables(C_ROPE, C_NOPE, LANES)
    cc = jnp.zeros((16, D_MODEL), _F32).at[:batch].set(c).at[batch].set(c_ctx)
    h = jnp.concatenate([x, ctx], axis=1)
    depth = len(layers)
    for i, (ada_w, ada_b, norms, w_o, w1, w2) in enumerate(layers):
        n_tiles = LATENT_TILES if i == depth - 1 else TILES_PER_SAMPLE
        mod = _ada(cc, ada_w, ada_b)
        mod_lat = mod[:batch].reshape(batch, 1, 6, D_MODEL)
        mod_ctx = jnp.broadcast_to(mod[batch].reshape(1, 1, 6, D_MODEL), (batch, 1, 6, D_MODEL))
        mod_all = jnp.concatenate([mod_lat, mod_ctx], axis=1)
        if i == 0:
            a, w_o = _layer_window(h, mod_all, norms, l0_w_qkv, l0_sink, w_o, n_tiles, tables_ab)
        elif i == 1:
            a, w_o = _layer_diff(h, mod_all, norms, l1_w_qkv, l1_lambda, l1_subln, w_o, i, n_tiles,
                                 tables_ab)
        elif i == 2:
            a, w_o = _layer_mla(h, mod_all, norms, l2_w_in, l2_q_norm, l2_kv_norm, l2_w_uq, l2_w_ukv,
                                w_o, n_tiles, tables_c)
        else:
            a, w_o = _layer_window(h, mod_all, norms, l3_w_qkv, l3_sink, w_o, n_tiles, tables_ab)
        h = _post(a, h, mod_all, norms, w_o.astype(_BF16), w1.astype(_BF16), w2.astype(_BF16), n_tiles)
    return h
```

```python
import functools
import math
from typing import Any, NamedTuple

import jax
import jax.numpy as jnp
import numpy as np
from jax import lax
from jax.experimental import pallas as pl
from jax.experimental.pallas import tpu as pltpu

D_MODEL = 1024
SEQ = 4096
CTX_LEN = 256
S_ALL = SEQ + CTX_LEN
GRID_W = 64
D_FF = 4 * D_MODEL
EPS = 1e-6
ROPE_THETA = 10000.0
NEG_INF = -1e30
WINDOW = 128
LOG2E = math.log2(math.e)

A_HEADS = 16
A_KV_HEADS = 4
A_GROUP = 4
A_HEAD_DIM = 64
B_HEADS = 8
B_HEAD_DIM = 64
C_HEADS = 16
C_Q_LORA = 384
C_KV_LORA = 256
C_NOPE = 64
C_ROPE = 32
C_V = 64

LANES = 128
MXU_W = 256
TOK_TILE = 256
TILES_PER_SAMPLE = S_ALL // TOK_TILE
LATENT_TILES = SEQ // TOK_TILE
PARTIALS = 2
VMEM_LIMIT = 56 * 1024 * 1024

_BF16 = jnp.bfloat16
_F32 = jnp.float32


def _params(n_parallel):
    return pltpu.CompilerParams(
        dimension_semantics=("parallel",) * n_parallel, vmem_limit_bytes=VMEM_LIMIT)


def _rms(x):
    return x * lax.rsqrt(jnp.mean(x * x, axis=-1, keepdims=True) + EPS)


def _dot(a, b):
    return jnp.dot(a, b, preferred_element_type=_F32)


def _dot_nt(a, b):
    return lax.dot_general(a, b, (((1,), (1,)), ((), ())), preferred_element_type=_F32)


def _rope_tables(d_rot, lane_off, period):
    q = d_rot // 4
    da = d_rot // 2
    inv = ROPE_THETA ** (-jnp.arange(0, da, 2, dtype=_F32) / da)
    t = jnp.arange(SEQ, dtype=jnp.int32)
    rows = (t // GRID_W).astype(_F32)[:, None] * inv[None, :]
    cols = (t % GRID_W).astype(_F32)[:, None] * inv[None, :]
    zero = jnp.zeros((SEQ, q), _F32)
    cos = jnp.concatenate([jnp.cos(rows), jnp.cos(rows), jnp.cos(cols), jnp.cos(cols)], axis=1)
    sin_up = jnp.concatenate([-jnp.sin(rows), zero, -jnp.sin(cols), zero], axis=1)
    sin_dn = jnp.concatenate([zero, jnp.sin(rows), zero, jnp.sin(cols)], axis=1)

    def place(tab, fill):
        unit = jnp.full((SEQ, period), fill, _F32).at[:, lane_off:lane_off + d_rot].set(tab)
        full = jnp.tile(unit, (1, MXU_W // period))
        ctx = jnp.full((CTX_LEN, MXU_W), fill, _F32)
        return jnp.concatenate([full, ctx], axis=0)

    return place(cos, 1.0), place(sin_up, 0.0), place(sin_dn, 0.0)


def _rope_block(x, cos, sin_up, sin_dn, q):
    width = x.shape[1]
    up = pltpu.roll(x, width - q, 1)
    dn = pltpu.roll(x, q, 1)
    return x * cos + up * sin_up + dn * sin_dn


def _ada_kernel(c_ref, w_ref, b_ref, o_ref):
    c = c_ref[...]
    s = c * (1.0 / (1.0 + jnp.exp(-c)))
    o_ref[...] = jnp.dot(s, w_ref[...], preferred_element_type=_F32,
                         precision=lax.Precision.HIGHEST) + b_ref[...]


def _ada(cc, w, b):
    n = w.shape[1]
    bn = D_MODEL
    return pl.pallas_call(
        _ada_kernel,
        grid=(n // bn,),
        in_specs=[pl.BlockSpec((16, D_MODEL), lambda j: (0, 0)),
                  pl.BlockSpec((D_MODEL, bn), lambda j: (0, j)),
                  pl.BlockSpec((1, bn), lambda j: (0, j))],
        out_specs=pl.BlockSpec((16, bn), lambda j: (0, j)),
        out_shape=jax.ShapeDtypeStruct((16, n), _F32),
        compiler_params=_params(1),
        name="ada",
    )(cc, w, b.reshape(1, n))


def _modulated(h_ref, mod_ref, g_ref, norm_row, shift_row):
    y = _rms(h_ref[...]) * g_ref[norm_row:norm_row + 1, :]
    return y * (1.0 + mod_ref[shift_row + 1:shift_row + 2, :]) + mod_ref[shift_row:shift_row + 1, :]


def _tok_spec(width):
    return pl.BlockSpec((None, TOK_TILE, width), lambda b, j: (b, j, 0))


def _mod_spec():
    return pl.BlockSpec((None, None, 6, D_MODEL), lambda b, j: (b, j // LATENT_TILES, 0, 0))


def _const_spec(shape):
    return pl.BlockSpec(shape, lambda b, j: (0,) * len(shape))


def _table_spec():
    return pl.BlockSpec((TOK_TILE, MXU_W), lambda b, j: (j, 0))


def _keys_t_spec(width):
    return pl.BlockSpec((None, None, width, TOK_TILE), lambda b, j: (b, j, 0, 0))


def _keys_t_shape(batch, width):
    return jax.ShapeDtypeStruct((batch, TILES_PER_SAMPLE, width, TOK_TILE), _BF16)


def _qkv_kernel(h_ref, mod_ref, g_ref, w_ref, cos_ref, up_ref, dn_ref, q_ref, k_ref, v_ref,
                *, nq, nk, qscale, keys_transposed):
    u = _modulated(h_ref, mod_ref, g_ref, 0, 0).astype(_BF16)
    cos, up, dn = cos_ref[...], up_ref[...], dn_ref[...]
    quarter = A_HEAD_DIM // 4
    for blk in range(nq // MXU_W):
        x = _dot(u, w_ref[:, blk * MXU_W:(blk + 1) * MXU_W])
        x = _rope_block(x, cos, up, dn, quarter) * qscale
        q_ref[:, blk * MXU_W:(blk + 1) * MXU_W] = x.astype(_BF16)
    for blk in range(nk // MXU_W):
        x = _dot(u, w_ref[:, nq + blk * MXU_W:nq + (blk + 1) * MXU_W])
        x = _rope_block(x, cos, up, dn, quarter)
        if keys_transposed:
            k_ref[blk * MXU_W:(blk + 1) * MXU_W, :] = x.T.astype(_BF16)
        else:
            k_ref[:, blk * MXU_W:(blk + 1) * MXU_W] = x.astype(_BF16)
    v_ref[...] = _dot(u, w_ref[:, nq + nk:]).astype(_BF16)


def _qkv_proj(h, mod_all, norms, w, tables, nq, nk, nv, qscale, keys_transposed):
    batch = h.shape[0]
    bf16_rows = lambda n: jax.ShapeDtypeStruct((batch, S_ALL, n), _BF16)
    return pl.pallas_call(
        functools.partial(_qkv_kernel, nq=nq, nk=nk, qscale=qscale, keys_transposed=keys_transposed),
        grid=(batch, TILES_PER_SAMPLE),
        in_specs=[_tok_spec(D_MODEL), _mod_spec(), _const_spec((4, D_MODEL)),
                  _const_spec((D_MODEL, nq + nk + nv)),
                  _table_spec(), _table_spec(), _table_spec()],
        out_specs=[_tok_spec(nq), _keys_t_spec(nk) if keys_transposed else _tok_spec(nk), _tok_spec(nv)],
        out_shape=[bf16_rows(nq), _keys_t_shape(batch, nk) if keys_transposed else bf16_rows(nk),
                   bf16_rows(nv)],
        compiler_params=_params(2),
        name="qkv_proj",
    )(h, mod_all, norms, w, *tables)


C_QK_PAD = LANES
C_IN_PAD = C_Q_LORA + C_KV_LORA + LANES


def _mla_proj_kernel(h_ref, mod_ref, g_ref, w_in_ref, qn_ref, kvn_ref, w_uq_ref, w_uk_ref, w_uv_ref,
                     cos_ref, up_ref, dn_ref, q_ref, k_ref, v_ref, *, qscale):
    u = _modulated(h_ref, mod_ref, g_ref, 0, 0).astype(_BF16)
    cos, up, dn = cos_ref[...], up_ref[...], dn_ref[...]
    quarter = C_ROPE // 4
    t = _dot(u, w_in_ref[...])
    cq = (_rms(t[:, :C_Q_LORA]) * qn_ref[...]).astype(_BF16)
    ckv = (_rms(t[:, C_Q_LORA:C_Q_LORA + C_KV_LORA]) * kvn_ref[...]).astype(_BF16)
    kr = _rope_block(t[:, C_Q_LORA + C_KV_LORA:], cos[:, :LANES], up[:, :LANES], dn[:, :LANES], quarter)
    kr = jnp.concatenate([kr] * (MXU_W // C_QK_PAD), axis=1)
    for blk in range(C_HEADS * C_QK_PAD // MXU_W):
        sl = slice(blk * MXU_W, (blk + 1) * MXU_W)
        q = _rope_block(_dot(cq, w_uq_ref[:, sl]), cos, up, dn, quarter) * qscale
        q_ref[:, sl] = q.astype(_BF16)
        k_ref[sl, :] = (_dot(ckv, w_uk_ref[:, sl]) + kr).T.astype(_BF16)
    v_ref[...] = _dot(ckv, w_uv_ref[...]).astype(_BF16)


def _mla_proj(h, mod_all, norms, w_in, q_norm, kv_norm, w_uq, w_uk, w_uv, tables, qscale):
    batch = h.shape[0]
    nqk = C_HEADS * C_QK_PAD
    nv = C_HEADS * C_V
    return pl.pallas_call(
        functools.partial(_mla_proj_kernel, qscale=qscale),
        grid=(batch, TILES_PER_SAMPLE),
        in_specs=[_tok_spec(D_MODEL), _mod_spec(), _const_spec((4, D_MODEL)),
                  _const_spec((D_MODEL, C_IN_PAD)), _const_spec((1, C_Q_LORA)),
                  _const_spec((1, C_KV_LORA)), _const_spec((C_Q_LORA, nqk)),
                  _const_spec((C_KV_LORA, nqk)), _const_spec((C_KV_LORA, nv)),
                  _table_spec(), _table_spec(), _table_spec()],
        out_specs=[_tok_spec(nqk), _keys_t_spec(nqk), _tok_spec(nv)],
        out_shape=[jax.ShapeDtypeStruct((batch, S_ALL, nqk), _BF16), _keys_t_shape(batch, nqk),
                   jax.ShapeDtypeStruct((batch, S_ALL, nv), _BF16)],
        compiler_params=_params(2),
        name="mla_proj",
    )(h, mod_all, norms, w_in, q_norm, kv_norm, w_uq, w_uk, w_uv, *tables)


A_BLOCK = A_KV_HEADS * A_HEAD_DIM
A_KEYS = 2 * TOK_TILE + CTX_LEN


def _window_kernel(sink_ref, q_ref, kp_ref, kc_ref, kn_ref, kx_ref, vp_ref, vc_ref, vn_ref, vx_ref,
                   o_ref):
    j = pl.program_id(1)
    k_all = jnp.concatenate([kp_ref[...], kc_ref[...], kn_ref[...], kx_ref[...]], axis=0)
    v_all = jnp.concatenate([vp_ref[...], vc_ref[...], vn_ref[...], vx_ref[...]], axis=0)
    qi = lax.broadcasted_iota(jnp.int32, (TOK_TILE, A_KEYS), 0)
    c = lax.broadcasted_iota(jnp.int32, (TOK_TILE, A_KEYS), 1)
    kpos = TOK_TILE * j - WINDOW + c
    local = ((c - qi >= 0) & (c - qi <= 2 * WINDOW) & (kpos >= 0) & (kpos < SEQ)
             & (j < LATENT_TILES))
    mask = jnp.concatenate([local | (c >= 2 * TOK_TILE)] * A_GROUP, axis=0)
    lane_group = lax.broadcasted_iota(jnp.int32, (TOK_TILE, A_BLOCK), 1) // A_HEAD_DIM
    out = [jnp.zeros((TOK_TILE, A_BLOCK), _F32) for _ in range(A_GROUP)]
    for g in range(A_KV_HEADS):
        sel = lane_group == g
        qg = jnp.concatenate(
            [jnp.where(sel, q_ref[:, r * A_BLOCK:(r + 1) * A_BLOCK], jnp.zeros((), _BF16))
             for r in range(A_GROUP)], axis=0)
        s = jnp.where(mask, _dot_nt(qg, k_all), NEG_INF)
        sink = jnp.concatenate(
            [jnp.full((TOK_TILE, 1), sink_ref[g * A_GROUP + r], _F32) for r in range(A_GROUP)], axis=0)
        m = jnp.maximum(jnp.max(s, axis=-1, keepdims=True), sink)
        p = jnp.exp(s - m)
        denom = jnp.sum(p, axis=-1, keepdims=True) + jnp.exp(sink - m)
        og = _dot(p.astype(_BF16), v_all) / denom
        for r in range(A_GROUP):
            out[r] = jnp.where(sel, og[r * TOK_TILE:(r + 1) * TOK_TILE], out[r])
    for r in range(A_GROUP):
        o_ref[:, r * A_BLOCK:(r + 1) * A_BLOCK] = out[r].astype(_BF16)


def _window_attention(q, k, v, sink, n_tiles):
    batch = q.shape[0]
    half = TOK_TILE // 2
    last_half = S_ALL // half - 1

    def half_spec(fn):
        return pl.BlockSpec((None, half, A_BLOCK), fn)

    prev_spec = half_spec(lambda b, j: (b, jnp.maximum(2 * j - 1, 0), 0))
    next_spec = half_spec(lambda b, j: (b, jnp.minimum(2 * j + 2, last_half), 0))
    cur_spec = pl.BlockSpec((None, TOK_TILE, A_BLOCK), lambda b, j: (b, j, 0))
    ctx_spec = pl.BlockSpec((None, CTX_LEN, A_BLOCK), lambda b, j: (b, LATENT_TILES, 0))
    kv_specs = [prev_spec, cur_spec, next_spec, ctx_spec]
    return pl.pallas_call(
        _window_kernel,
        grid=(batch, n_tiles),
        in_specs=[pl.BlockSpec(memory_space=pltpu.SMEM), _tok_spec(D_MODEL)] + kv_specs + kv_specs,
        out_specs=_tok_spec(D_MODEL),
        out_shape=jax.ShapeDtypeStruct((batch, n_tiles * TOK_TILE, D_MODEL), _BF16),
        compiler_params=_params(2),
        name="window_attention",
    )(sink, q, k, k, k, k, v, v, v, v)


STACK_ROWS = 2 * TOK_TILE
KEY_TILE = MXU_W
N_KEY_TILES = S_ALL // KEY_TILE
FUSED_TRIPS = N_KEY_TILES - 1
EXP_ROWS = STACK_ROWS // FUSED_TRIPS
FUSED_UNROLL = 4


def _stack_masked(q, split):
    first = lax.broadcasted_iota(jnp.int32, q.shape, 1) < split
    zero = jnp.zeros((), _BF16)
    return jnp.concatenate([jnp.where(first, q, zero), jnp.where(first, zero, q)], axis=0)


def _tile_rows(t):
    if isinstance(t, int):
        return pl.ds(t * KEY_TILE, KEY_TILE)
    return pl.ds(pl.multiple_of(t * KEY_TILE, KEY_TILE), KEY_TILE)


class _TileScratch(NamedTuple):
    q: Any
    s: Any
    m: Any
    p: Any
    l: Any
    acc: Any


def _qk_tile(tile, t, k_ref):
    keys_t = k_ref[t]
    for half in range(2):
        rows = pl.ds(half * TOK_TILE, TOK_TILE)
        s = _dot(tile.q[rows, :], keys_t)
        tile.s[t, rows, :] = s
        tile.m[rows, :] = jnp.maximum(tile.m[rows, :], jnp.maximum(s[:, :LANES], s[:, LANES:]))


def _qk_all(tile, k_ref):
    keys_t = jnp.concatenate([k_ref[t] for t in range(N_KEY_TILES)], axis=1)
    for half in range(2):
        rows = pl.ds(half * TOK_TILE, TOK_TILE)
        s = _dot(tile.q[rows, :], keys_t)
        m = None
        for t in range(N_KEY_TILES):
            st = s[:, t * KEY_TILE:(t + 1) * KEY_TILE]
            tile.s[t, rows, :] = st
            mt = jnp.maximum(st[:, :LANES], st[:, LANES:])
            m = mt if m is None else jnp.maximum(m, mt)
        tile.m[rows, :] = m


def _pv_all(tile, v_ref):
    for half in range(2):
        rows = pl.ds(half * TOK_TILE, TOK_TILE)
        p = jnp.concatenate([tile.p[t, rows, :] for t in range(N_KEY_TILES)], axis=1)
        tile.acc[rows, :] = _dot(p, v_ref[...])


def _reduce_max(tile):
    tile.m[...] = jnp.broadcast_to(jnp.max(tile.m[...], axis=-1, keepdims=True), tile.m.shape)


def _exp_rows(tile, i):
    rs = pl.ds(pl.multiple_of(i * EXP_ROWS, EXP_ROWS), EXP_ROWS)
    m = tile.m[rs, :]
    parts = [None] * PARTIALS
    n = 0
    for c in range(N_KEY_TILES):
        for lo in range(0, KEY_TILE, LANES):
            p = jnp.exp2(tile.s[c, rs, lo:lo + LANES] - m)
            parts[n % PARTIALS] = p if parts[n % PARTIALS] is None else parts[n % PARTIALS] + p
            tile.p[c, rs, lo:lo + LANES] = p.astype(_BF16)
            n += 1
    tile.l[rs, :] = functools.reduce(jnp.add, parts)


def _pv_chunk(tile, t, v_ref):
    vals = v_ref[_tile_rows(t), :]
    for half in range(2):
        rows = pl.ds(half * TOK_TILE, TOK_TILE)
        tile.acc[rows, :] += _dot(tile.p[t, rows, :], vals)


def _attend_pair(tiles, k_ref, v_ref):
    first, second = tiles
    second.m[...] = jnp.full(second.m.shape, -jnp.inf, _F32)
    first.acc[...] = jnp.zeros(first.acc.shape, _F32)
    _qk_all(first, k_ref)
    _reduce_max(first)

    def exp_first_scores_second(t, carry):
        _exp_rows(first, t)
        _qk_tile(second, t, k_ref)
        return carry

    lax.fori_loop(0, FUSED_TRIPS, exp_first_scores_second, 0, unroll=FUSED_UNROLL)
    _qk_tile(second, FUSED_TRIPS, k_ref)
    _reduce_max(second)

    def out_first_exp_second(t, carry):
        _pv_chunk(first, t, v_ref)
        _exp_rows(second, t)
        return carry

    lax.fori_loop(0, FUSED_TRIPS, out_first_exp_second, 0, unroll=FUSED_UNROLL)
    _pv_chunk(first, FUSED_TRIPS, v_ref)
    _pv_all(second, v_ref)
    return [tile.acc[...] / jnp.sum(tile.l[...], axis=-1, keepdims=True) for tile in tiles]


def _pair_scratch(qk_width, v_width):
    one = [pltpu.VMEM((STACK_ROWS, qk_width), _BF16),
           pltpu.VMEM((N_KEY_TILES, STACK_ROWS, KEY_TILE), _F32),
           pltpu.VMEM((STACK_ROWS, LANES), _F32),
           pltpu.VMEM((N_KEY_TILES, STACK_ROWS, KEY_TILE), _BF16),
           pltpu.VMEM((STACK_ROWS, LANES), _F32),
           pltpu.VMEM((STACK_ROWS, v_width), _F32)]
    return one + one


def _pair_tiles(scratch):
    n = len(_TileScratch._fields)
    return _TileScratch(*scratch[:n]), _TileScratch(*scratch[n:])


def _attend_context(q, k_ref, v_ref):
    s = _dot(q, k_ref[...])
    p = jnp.exp2(s - jnp.max(s, axis=-1, keepdims=True))
    return _dot(p.astype(_BF16), v_ref[...]) / jnp.sum(p, axis=-1, keepdims=True)


def _dense_attention(latent_kernel, context_kernel, params, q, k, v, heads, qk_width, v_width,
                     with_context, name):
    assert KEY_TILE == TOK_TILE == CTX_LEN
    batch = q.shape[0]
    pair = 2 * TOK_TILE
    n_rows = S_ALL if with_context else SEQ
    out_shape = jax.ShapeDtypeStruct((batch, n_rows, heads * v_width), _BF16)

    param_specs = [pl.BlockSpec(p.shape, lambda *_, nd=p.ndim: (0,) * nd) for p in params]

    out = pl.pallas_call(
        latent_kernel,
        grid=(batch, heads, SEQ // pair),
        in_specs=param_specs + [
            pl.BlockSpec((None, pair, qk_width), lambda b, h, j: (b, j, h)),
            pl.BlockSpec((None, N_KEY_TILES, qk_width, KEY_TILE), lambda b, h, j: (b, 0, h, 0)),
            pl.BlockSpec((None, S_ALL, v_width), lambda b, h, j: (b, 0, h))],
        out_specs=pl.BlockSpec((None, pair, v_width), lambda b, h, j: (b, j, h)),
        out_shape=out_shape,
        scratch_shapes=_pair_scratch(qk_width, v_width),
        compiler_params=_params(3),
        name=name,
    )(*params, q, k, v)
    if not with_context:
        return out
    ctx_block = SEQ // CTX_LEN
    return pl.pallas_call(
        context_kernel,
        grid=(batch, heads),
        in_specs=param_specs + [
            pl.BlockSpec((None, CTX_LEN, qk_width), lambda b, h: (b, ctx_block, h)),
            pl.BlockSpec((None, None, qk_width, KEY_TILE), lambda b, h: (b, ctx_block, h, 0)),
            pl.BlockSpec((None, CTX_LEN, v_width), lambda b, h: (b, ctx_block, h)),
            pl.BlockSpec(memory_space=pl.ANY)],
        out_specs=pl.BlockSpec((None, CTX_LEN, v_width), lambda b, h: (b, ctx_block, h)),
        out_shape=out_shape,
        input_output_aliases={len(params) + 3: 0},
        compiler_params=_params(2),
        name=name + "_context",
    )(*params, q, k, v, out)


def _diff_finish(a, lam_ref, subln_ref, lam_init):
    lam = lam_ref[...]
    lam_full = (jnp.exp(jnp.sum(lam[0:1] * lam[1:2], axis=-1, keepdims=True))
                - jnp.exp(jnp.sum(lam[2:3] * lam[3:4], axis=-1, keepdims=True)) + lam_init)
    o = a[:TOK_TILE] - lam_full * a[TOK_TILE:]
    return (_rms(o) * subln_ref[...] * (1.0 - lam_init)).astype(_BF16)


def _diff_latent_kernel(lam_ref, subln_ref, q_ref, k_ref, v_ref, o_ref, *scratch, lam_init):
    tiles = _pair_tiles(scratch)
    for x in range(2):
        tiles[x].q[...] = _stack_masked(q_ref[x * TOK_TILE:(x + 1) * TOK_TILE, :], B_HEAD_DIM)
    outs = _attend_pair(tiles, k_ref, v_ref)
    for x in range(2):
        o_ref[x * TOK_TILE:(x + 1) * TOK_TILE, :] = _diff_finish(outs[x], lam_ref, subln_ref, lam_init)


def _diff_context_kernel(lam_ref, subln_ref, q_ref, k_ref, v_ref, buf_ref, o_ref, *, lam_init):
    del buf_ref
    a = _attend_context(_stack_masked(q_ref[...], B_HEAD_DIM), k_ref, v_ref)
    o_ref[...] = _diff_finish(a, lam_ref, subln_ref, lam_init)


def _diff_attention(q, k, v, lam, subln, lam_init, with_context):
    head_w = 2 * B_HEAD_DIM
    return _dense_attention(functools.partial(_diff_latent_kernel, lam_init=lam_init),
                            functools.partial(_diff_context_kernel, lam_init=lam_init),
                            (lam, subln), q, k, v, B_HEADS, head_w, head_w, with_context,
                            "diff_attention")


def _mla_finish(a):
    first_v = lax.broadcasted_iota(jnp.int32, (TOK_TILE, 2 * C_V), 1) < C_V
    return jnp.where(first_v, a[:TOK_TILE], a[TOK_TILE:]).astype(_BF16)


def _mla_latent_kernel(q_ref, k_ref, v_ref, o_ref, *scratch):
    tiles = _pair_tiles(scratch)
    for x in range(2):
        tiles[x].q[...] = _stack_masked(q_ref[x * TOK_TILE:(x + 1) * TOK_TILE, :], C_QK_PAD)
    outs = _attend_pair(tiles, k_ref, v_ref)
    for x in range(2):
        o_ref[x * TOK_TILE:(x + 1) * TOK_TILE, :] = _mla_finish(outs[x])


def _mla_context_kernel(q_ref, k_ref, v_ref, buf_ref, o_ref):
    del buf_ref
    o_ref[...] = _mla_finish(_attend_context(_stack_masked(q_ref[...], C_QK_PAD), k_ref, v_ref))


def _mla_attention(q, k, v, with_context):
    return _dense_attention(_mla_latent_kernel, _mla_context_kernel, (), q, k, v, C_HEADS // 2,
                            2 * C_QK_PAD, 2 * C_V, with_context, "mla_attention")


def _post_kernel(a_ref, h_ref, mod_ref, g_ref, wo_ref, w1_ref, w2_ref, o_ref):
    y = _dot(a_ref[...], wo_ref[...])
    h1 = h_ref[...] + mod_ref[2:3, :] * (_rms(y) * g_ref[1:2, :])
    u = (_rms(h1) * g_ref[2:3, :]) * (1.0 + mod_ref[4:5, :]) + mod_ref[3:4, :]
    t = jnp.maximum(_dot(u.astype(_BF16), w1_ref[...]), 0.0)
    z = _dot((t * t).astype(_BF16), w2_ref[...])
    o_ref[...] = h1 + mod_ref[5:6, :] * (_rms(z) * g_ref[3:4, :])


def _post(a, h, mod_all, norms, w_o, w1, w2, n_tiles):
    batch = h.shape[0]
    return pl.pallas_call(
        _post_kernel,
        grid=(batch, n_tiles),
        in_specs=[_tok_spec(D_MODEL), _tok_spec(D_MODEL), _mod_spec(), _const_spec((4, D_MODEL)),
                  _const_spec((D_MODEL, D_MODEL)), _const_spec((D_MODEL, D_FF)),
                  _const_spec((D_FF, D_MODEL))],
        out_specs=_tok_spec(D_MODEL),
        out_shape=jax.ShapeDtypeStruct((batch, n_tiles * TOK_TILE, D_MODEL), _F32),
        compiler_params=_params(2),
        name="post",
    )(a, h, mod_all, norms, w_o, w1, w2)


def _a_head_perm():
    new = np.arange(A_HEADS * A_HEAD_DIM)
    r, g, d = new // A_BLOCK, (new % A_BLOCK) // A_HEAD_DIM, new % A_HEAD_DIM
    return (g * A_GROUP + r) * A_HEAD_DIM + d


def _layer_window(h, mod_all, norms, w_qkv, sink, w_o, n_tiles, tables):
    nq = A_HEADS * A_HEAD_DIM
    nk = A_KV_HEADS * A_HEAD_DIM
    perm = _a_head_perm()
    w = jnp.concatenate([w_qkv[:, :nq][:, perm], w_qkv[:, nq:]], axis=1).astype(_BF16)
    q, k, v = _qkv_proj(h, mod_all, norms, w, tables, nq, nk, nk, A_HEAD_DIM ** -0.5, False)
    a = _window_attention(q, k, v, sink, n_tiles)
    return a, w_o[perm, :]


def _layer_diff(h, mod_all, norms, w_qkv, lam, subln, w_o, layer_idx, n_tiles, tables):
    n = B_HEADS * 2 * B_HEAD_DIM
    lam_init = 0.8 - 0.6 * math.exp(-0.3 * layer_idx)
    q, k, v = _qkv_proj(h, mod_all, norms, w_qkv.astype(_BF16), tables, n, n, n,
                        B_HEAD_DIM ** -0.5 * LOG2E, True)
    a = _diff_attention(q, k, v, lam, subln.reshape(1, -1), lam_init, n_tiles == TILES_PER_SAMPLE)
    return a, w_o


def _layer_mla(h, mod_all, norms, w_in, q_norm, kv_norm, w_uq, w_ukv, w_o, n_tiles, tables):
    dqk = C_NOPE + C_ROPE
    zeros = functools.partial(jnp.zeros, dtype=_F32)
    w_in_p = jnp.concatenate(
        [w_in[:, :C_Q_LORA + C_KV_LORA], zeros((D_MODEL, C_NOPE)), w_in[:, C_Q_LORA + C_KV_LORA:],
         zeros((D_MODEL, C_QK_PAD - dqk))], axis=1).astype(_BF16)
    w_uq_p = jnp.pad(w_uq.reshape(C_Q_LORA, C_HEADS, dqk), ((0, 0), (0, 0), (0, C_QK_PAD - dqk)))
    w_uq_p = w_uq_p.reshape(C_Q_LORA, C_HEADS * C_QK_PAD).astype(_BF16)
    w_ukv_h = w_ukv.reshape(C_KV_LORA, C_HEADS, C_NOPE + C_V)
    w_uk_p = jnp.pad(w_ukv_h[:, :, :C_NOPE], ((0, 0), (0, 0), (0, C_QK_PAD - C_NOPE)))
    w_uk_p = w_uk_p.reshape(C_KV_LORA, C_HEADS * C_QK_PAD).astype(_BF16)
    w_uv = w_ukv_h[:, :, C_NOPE:].reshape(C_KV_LORA, C_HEADS * C_V).astype(_BF16)
    q, k, v = _mla_proj(h, mod_all, norms, w_in_p, q_norm.reshape(1, -1), kv_norm.reshape(1, -1),
                        w_uq_p, w_uk_p, w_uv, tables, dqk ** -0.5 * LOG2E)
    a = _mla_attention(q, k, v, n_tiles == TILES_PER_SAMPLE)
    return a, w_o


def kernel(x, c, ctx, c_ctx, l0_ada_w, l0_ada_b, l0_norms, l0_w_qkv, l0_sink, l0_w_o, l0_mlp_w1, l0_mlp_w2, l1_ada_w, l1_ada_b, l1_norms, l1_w_qkv, l1_lambda, l1_subln, l1_w_o, l1_mlp_w1, l1_mlp_w2, l2_ada_w, l2_ada_b, l2_norms, l2_w_in, l2_q_norm, l2_kv_norm, l2_w_uq, l2_w_ukv, l2_w_o, l2_mlp_w1, l2_mlp_w2, l3_ada_w, l3_ada_b, l3_norms, l3_w_qkv, l3_sink, l3_w_o, l3_mlp_w1, l3_mlp_w2):
    batch = x.shape[0]
    assert x.shape == (batch, SEQ, D_MODEL) and ctx.shape == (batch, CTX_LEN, D_MODEL) and batch < 16
    layers = [
        (l0_ada_w, l0_ada_b, l0_norms, l0_w_o, l0_mlp_w1, l0_mlp_w2),
        (l1_ada_w, l1_ada_b, l1_norms, l1_w_o, l1_mlp_w1, l1_mlp_w2),
        (l2_ada_w, l2_ada_b, l2_norms, l2_w_o, l2_mlp_w1, l2_mlp_w2),
        (l3_ada_w, l3_ada_b, l3_norms, l3_w_o, l3_mlp_w1, l3_mlp_w2),
    ]
    tables_ab = _rope_tables(A_HEAD_DIM, 0, A_HEAD_DIM)
    tables_c = _rope_tables(C_ROPE, C_NOPE, LANES)
    cc = jnp.zeros((16, D_MODEL), _F32).at[:batch].set(c).at[batch].set(c_ctx)
    h = jnp.concatenate([x, ctx], axis=1)
    depth = len(layers)
    for i, (ada_w, ada_b, norms, w_o, w1, w2) in enumerate(layers):
        n_tiles = LATENT_TILES if i == depth - 1 else TILES_PER_SAMPLE
        mod = _ada(cc, ada_w, ada_b)
        mod_lat = mod[:batch].reshape(batch, 1, 6, D_MODEL)
        mod_ctx = jnp.broadcast_to(mod[batch].reshape(1, 1, 6, D_MODEL), (batch, 1, 6, D_MODEL))
        mod_all = jnp.concatenate([mod_lat, mod_ctx], axis=1)
        if i == 0:
            a, w_o = _layer_window(h, mod_all, norms, l0_w_qkv, l0_sink, w_o, n_tiles, tables_ab)
        elif i == 1:
            a, w_o = _layer_diff(h, mod_all, norms, l1_w_qkv, l1_lambda, l1_subln, w_o, i, n_tiles,
                                 tables_ab)
        elif i == 2:
            a, w_o = _layer_mla(h, mod_all, norms, l2_w_in, l2_q_norm, l2_kv_norm, l2_w_uq, l2_w_ukv,
                                w_o, n_tiles, tables_c)
        else:
            a, w_o = _layer_window(h, mod_all, norms, l3_w_qkv, l3_sink, w_o, n_tiles, tables_ab)
        h = _post(a, h, mod_all, norms, w_o.astype(_BF16), w1.astype(_BF16), w2.astype(_BF16), n_tiles)
    return h
```

```python
import functools
import math
from typing import Any, NamedTuple

import jax
import jax.numpy as jnp
import numpy as np
from jax import lax
from jax.experimental import pallas as pl
from jax.experimental.pallas import tpu as pltpu

D_MODEL = 1024
SEQ = 4096
CTX_LEN = 256
S_ALL = SEQ + CTX_LEN
GRID_W = 64
D_FF = 4 * D_MODEL
EPS = 1e-6
ROPE_THETA = 10000.0
NEG_INF = -1e30
WINDOW = 128
LOG2E = math.log2(math.e)

A_HEADS = 16
A_KV_HEADS = 4
A_GROUP = 4
A_HEAD_DIM = 64
B_HEADS = 8
B_HEAD_DIM = 64
C_HEADS = 16
C_Q_LORA = 384
C_KV_LORA = 256
C_NOPE = 64
C_ROPE = 32
C_V = 64

LANES = 128
MXU_W = 256
TOK_TILE = 256
TILES_PER_SAMPLE = S_ALL // TOK_TILE
LATENT_TILES = SEQ // TOK_TILE
PARTIALS = 2
VMEM_LIMIT = 56 * 1024 * 1024

_BF16 = jnp.bfloat16
_F32 = jnp.float32


def _params(n_parallel):
    return pltpu.CompilerParams(
        dimension_semantics=("parallel",) * n_parallel, vmem_limit_bytes=VMEM_LIMIT)


def _rms(x):
    return x * lax.rsqrt(jnp.mean(x * x, axis=-1, keepdims=True) + EPS)


def _dot(a, b):
    return jnp.dot(a, b, preferred_element_type=_F32)


def _dot_nt(a, b):
    return lax.dot_general(a, b, (((1,), (1,)), ((), ())), preferred_element_type=_F32)


def _rope_tables(d_rot, lane_off, period):
    q = d_rot // 4
    da = d_rot // 2
    inv = ROPE_THETA ** (-jnp.arange(0, da, 2, dtype=_F32) / da)
    t = jnp.arange(SEQ, dtype=jnp.int32)
    rows = (t // GRID_W).astype(_F32)[:, None] * inv[None, :]
    cols = (t % GRID_W).astype(_F32)[:, None] * inv[None, :]
    zero = jnp.zeros((SEQ, q), _F32)
    cos = jnp.concatenate([jnp.cos(rows), jnp.cos(rows), jnp.cos(cols), jnp.cos(cols)], axis=1)
    sin_up = jnp.concatenate([-jnp.sin(rows), zero, -jnp.sin(cols), zero], axis=1)
    sin_dn = jnp.concatenate([zero, jnp.sin(rows), zero, jnp.sin(cols)], axis=1)

    def place(tab, fill):
        unit = jnp.full((SEQ, period), fill, _F32).at[:, lane_off:lane_off + d_rot].set(tab)
        full = jnp.tile(unit, (1, MXU_W // period))
        ctx = jnp.full((CTX_LEN, MXU_W), fill, _F32)
        return jnp.concatenate([full, ctx], axis=0)

    return place(cos, 1.0), place(sin_up, 0.0), place(sin_dn, 0.0)


def _rope_block(x, cos, sin_up, sin_dn, q):
    width = x.shape[1]
    up = pltpu.roll(x, width - q, 1)
    dn = pltpu.roll(x, q, 1)
    return x * cos + up * sin_up + dn * sin_dn


def _ada_kernel(c_ref, w_ref, b_ref, o_ref):
    c = c_ref[...]
    s = c * (1.0 / (1.0 + jnp.exp(-c)))
    o_ref[...] = jnp.dot(s, w_ref[...], preferred_element_type=_F32,
                         precision=lax.Precision.HIGHEST) + b_ref[...]


def _ada(cc, w, b):
    n = w.shape[1]
    bn = D_MODEL
    return pl.pallas_call(
        _ada_kernel,
        grid=(n // bn,),
        in_specs=[pl.BlockSpec((16, D_MODEL), lambda j: (0, 0)),
                  pl.BlockSpec((D_MODEL, bn), lambda j: (0, j)),
                  pl.BlockSpec((1, bn), lambda j: (0, j))],
        out_specs=pl.BlockSpec((16, bn), lambda j: (0, j)),
        out_shape=jax.ShapeDtypeStruct((16, n), _F32),
        compiler_params=_params(1),
        name="ada",
    )(cc, w, b.reshape(1, n))


def _modulated(h_ref, mod_ref, g_ref, norm_row, shift_row):
    y = _rms(h_ref[...]) * g_ref[norm_row:norm_row + 1, :]
    return y * (1.0 + mod_ref[shift_row + 1:shift_row + 2, :]) + mod_ref[shift_row:shift_row + 1, :]


def _tok_spec(width):
    return pl.BlockSpec((None, TOK_TILE, width), lambda b, j: (b, j, 0))


def _mod_spec():
    return pl.BlockSpec((None, None, 6, D_MODEL), lambda b, j: (b, j // LATENT_TILES, 0, 0))


def _const_spec(shape):
    return pl.BlockSpec(shape, lambda b, j: (0,) * len(shape))


def _table_spec():
    return pl.BlockSpec((TOK_TILE, MXU_W), lambda b, j: (j, 0))


def _keys_t_spec(width):
    return pl.BlockSpec((None, None, width, TOK_TILE), lambda b, j: (b, j, 0, 0))


def _keys_t_shape(batch, width):
    return jax.ShapeDtypeStruct((batch, TILES_PER_SAMPLE, width, TOK_TILE), _BF16)


def _qkv_kernel(h_ref, mod_ref, g_ref, w_ref, cos_ref, up_ref, dn_ref, q_ref, k_ref, v_ref,
                *, nq, nk, qscale, keys_transposed):
    u = _modulated(h_ref, mod_ref, g_ref, 0, 0).astype(_BF16)
    cos, up, dn = cos_ref[...], up_ref[...], dn_ref[...]
    quarter = A_HEAD_DIM // 4
    for blk in range(nq // MXU_W):
        x = _dot(u, w_ref[:, blk * MXU_W:(blk + 1) * MXU_W])
        x = _rope_block(x, cos, up, dn, quarter) * qscale
        q_ref[:, blk * MXU_W:(blk + 1) * MXU_W] = x.astype(_BF16)
    for blk in range(nk // MXU_W):
        x = _dot(u, w_ref[:, nq + blk * MXU_W:nq + (blk + 1) * MXU_W])
        x = _rope_block(x, cos, up, dn, quarter)
        if keys_transposed:
            k_ref[blk * MXU_W:(blk + 1) * MXU_W, :] = x.T.astype(_BF16)
        else:
            k_ref[:, blk * MXU_W:(blk + 1) * MXU_W] = x.astype(_BF16)
    v_ref[...] = _dot(u, w_ref[:, nq + nk:]).astype(_BF16)


def _qkv_proj(h, mod_all, norms, w, tables, nq, nk, nv, qscale, keys_transposed):
    batch = h.shape[0]
    bf16_rows = lambda n: jax.ShapeDtypeStruct((batch, S_ALL, n), _BF16)
    return pl.pallas_call(
        functools.partial(_qkv_kernel, nq=nq, nk=nk, qscale=qscale, keys_transposed=keys_transposed),
        grid=(batch, TILES_PER_SAMPLE),
        in_specs=[_tok_spec(D_MODEL), _mod_spec(), _const_spec((4, D_MODEL)),
                  _const_spec((D_MODEL, nq + nk + nv)),
                  _table_spec(), _table_spec(), _table_spec()],
        out_specs=[_tok_spec(nq), _keys_t_spec(nk) if keys_transposed else _tok_spec(nk), _tok_spec(nv)],
        out_shape=[bf16_rows(nq), _keys_t_shape(batch, nk) if keys_transposed else bf16_rows(nk),
                   bf16_rows(nv)],
        compiler_params=_params(2),
        name="qkv_proj",
    )(h, mod_all, norms, w, *tables)


C_QK_PAD = LANES
C_IN_PAD = C_Q_LORA + C_KV_LORA + LANES


def _mla_proj_kernel(h_ref, mod_ref, g_ref, w_in_ref, qn_ref, kvn_ref, w_uq_ref, w_uk_ref, w_uv_ref,
                     cos_ref, up_ref, dn_ref, q_ref, k_ref, v_ref, *, qscale):
    u = _modulated(h_ref, mod_ref, g_ref, 0, 0).astype(_BF16)
    cos, up, dn = cos_ref[...], up_ref[...], dn_ref[...]
    quarter = C_ROPE // 4
    t = _dot(u, w_in_ref[...])
    cq = (_rms(t[:, :C_Q_LORA]) * qn_ref[...]).astype(_BF16)
    ckv = (_rms(t[:, C_Q_LORA:C_Q_LORA + C_KV_LORA]) * kvn_ref[...]).astype(_BF16)
    kr = _rope_block(t[:, C_Q_LORA + C_KV_LORA:], cos[:, :LANES], up[:, :LANES], dn[:, :LANES], quarter)
    kr = jnp.concatenate([kr] * (MXU_W // C_QK_PAD), axis=1)
    for blk in range(C_HEADS * C_QK_PAD // MXU_W):
        sl = slice(blk * MXU_W, (blk + 1) * MXU_W)
        q = _rope_block(_dot(cq, w_uq_ref[:, sl]), cos, up, dn, quarter) * qscale
        q_ref[:, sl] = q.astype(_BF16)
        k_ref[sl, :] = (_dot(ckv, w_uk_ref[:, sl]) + kr).T.astype(_BF16)
    v_ref[...] = _dot(ckv, w_uv_ref[...]).astype(_BF16)


def _mla_proj(h, mod_all, norms, w_in, q_norm, kv_norm, w_uq, w_uk, w_uv, tables, qscale):
    batch = h.shape[0]
    nqk = C_HEADS * C_QK_PAD
    nv = C_HEADS * C_V
    return pl.pallas_call(
        functools.partial(_mla_proj_kernel, qscale=qscale),
        grid=(batch, TILES_PER_SAMPLE),
        in_specs=[_tok_spec(D_MODEL), _mod_spec(), _const_spec((4, D_MODEL)),
                  _const_spec((D_MODEL, C_IN_PAD)), _const_spec((1, C_Q_LORA)),
                  _const_spec((1, C_KV_LORA)), _const_spec((C_Q_LORA, nqk)),
                  _const_spec((C_KV_LORA, nqk)), _const_spec((C_KV_LORA, nv)),
                  _table_spec(), _table_spec(), _table_spec()],
        out_specs=[_tok_spec(nqk), _keys_t_spec(nqk), _tok_spec(nv)],
        out_shape=[jax.ShapeDtypeStruct((batch, S_ALL, nqk), _BF16), _keys_t_shape(batch, nqk),
                   jax.ShapeDtypeStruct((batch, S_ALL, nv), _BF16)],
        compiler_params=_params(2),
        name="mla_proj",
    )(h, mod_all, norms, w_in, q_norm, kv_norm, w_uq, w_uk, w_uv, *tables)


A_BLOCK = A_KV_HEADS * A_HEAD_DIM
A_KEYS = 2 * TOK_TILE + CTX_LEN


def _window_kernel(sink_ref, q_ref, kp_ref, kc_ref, kn_ref, kx_ref, vp_ref, vc_ref, vn_ref, vx_ref,
                   o_ref):
    j = pl.program_id(1)
    k_all = jnp.concatenate([kp_ref[...], kc_ref[...], kn_ref[...], kx_ref[...]], axis=0)
    v_all = jnp.concatenate([vp_ref[...], vc_ref[...], vn_ref[...], vx_ref[...]], axis=0)
    qi = lax.broadcasted_iota(jnp.int32, (TOK_TILE, A_KEYS), 0)
    c = lax.broadcasted_iota(jnp.int32, (TOK_TILE, A_KEYS), 1)
    kpos = TOK_TILE * j - WINDOW + c
    local = ((c - qi >= 0) & (c - qi <= 2 * WINDOW) & (kpos >= 0) & (kpos < SEQ)
             & (j < LATENT_TILES))
    mask = jnp.concatenate([local | (c >= 2 * TOK_TILE)] * A_GROUP, axis=0)
    lane_group = lax.broadcasted_iota(jnp.int32, (TOK_TILE, A_BLOCK), 1) // A_HEAD_DIM
    out = [jnp.zeros((TOK_TILE, A_BLOCK), _F32) for _ in range(A_GROUP)]
    for g in range(A_KV_HEADS):
        sel = lane_group == g
        qg = jnp.concatenate(
            [jnp.where(sel, q_ref[:, r * A_BLOCK:(r + 1) * A_BLOCK], jnp.zeros((), _BF16))
             for r in range(A_GROUP)], axis=0)
        s = jnp.where(mask, _dot_nt(qg, k_all), NEG_INF)
        sink = jnp.concatenate(
            [jnp.full((TOK_TILE, 1), sink_ref[g * A_GROUP + r], _F32) for r in range(A_GROUP)], axis=0)
        m = jnp.maximum(jnp.max(s, axis=-1, keepdims=True), sink)
        p = jnp.exp(s - m)
        denom = jnp.sum(p, axis=-1, keepdims=True) + jnp.exp(sink - m)
        og = _dot(p.astype(_BF16), v_all) / denom
        for r in range(A_GROUP):
            out[r] = jnp.where(sel, og[r * TOK_TILE:(r + 1) * TOK_TILE], out[r])
    for r in range(A_GROUP):
        o_ref[:, r * A_BLOCK:(r + 1) * A_BLOCK] = out[r].astype(_BF16)


def _window_attention(q, k, v, sink, n_tiles):
    batch = q.shape[0]
    half = TOK_TILE // 2
    last_half = S_ALL // half - 1

    def half_spec(fn):
        return pl.BlockSpec((None, half, A_BLOCK), fn)

    prev_spec = half_spec(lambda b, j: (b, jnp.maximum(2 * j - 1, 0), 0))
    next_spec = half_spec(lambda b, j: (b, jnp.minimum(2 * j + 2, last_half), 0))
    cur_spec = pl.BlockSpec((None, TOK_TILE, A_BLOCK), lambda b, j: (b, j, 0))
    ctx_spec = pl.BlockSpec((None, CTX_LEN, A_BLOCK), lambda b, j: (b, LATENT_TILES, 0))
    kv_specs = [prev_spec, cur_spec, next_spec, ctx_spec]
    return pl.pallas_call(
        _window_kernel,
        grid=(batch, n_tiles),
        in_specs=[pl.BlockSpec(memory_space=pltpu.SMEM), _tok_spec(D_MODEL)] + kv_specs + kv_specs,
        out_specs=_tok_spec(D_MODEL),
        out_shape=jax.ShapeDtypeStruct((batch, n_tiles * TOK_TILE, D_MODEL), _BF16),
        compiler_params=_params(2),
        name="window_attention",
    )(sink, q, k, k, k, k, v, v, v, v)


STACK_ROWS = 2 * TOK_TILE
KEY_TILE = MXU_W
N_KEY_TILES = S_ALL // KEY_TILE
FUSED_TRIPS = N_KEY_TILES - 1
EXP_ROWS = STACK_ROWS // FUSED_TRIPS
FUSED_UNROLL = 8


def _stack_masked(q, split):
    first = lax.broadcasted_iota(jnp.int32, q.shape, 1) < split
    zero = jnp.zeros((), _BF16)
    return jnp.concatenate([jnp.where(first, q, zero), jnp.where(first, zero, q)], axis=0)


def _tile_rows(t):
    if isinstance(t, int):
        return pl.ds(t * KEY_TILE, KEY_TILE)
    return pl.ds(pl.multiple_of(t * KEY_TILE, KEY_TILE), KEY_TILE)


class _TileScratch(NamedTuple):
    q: Any
    s: Any
    m: Any
    p: Any
    l: Any
    acc: Any


def _qk_tile(tile, t, k_ref):
    keys_t = k_ref[t]
    for half in range(2):
        rows = pl.ds(half * TOK_TILE, TOK_TILE)
        s = _dot(tile.q[rows, :], keys_t)
        tile.s[t, rows, :] = s
        tile.m[rows, :] = jnp.maximum(tile.m[rows, :], jnp.maximum(s[:, :LANES], s[:, LANES:]))


def _qk_all(tile, k_ref):
    keys_t = jnp.concatenate([k_ref[t] for t in range(N_KEY_TILES)], axis=1)
    for half in range(2):
        rows = pl.ds(half * TOK_TILE, TOK_TILE)
        s = _dot(tile.q[rows, :], keys_t)
        m = None
        for t in range(N_KEY_TILES):
            st = s[:, t * KEY_TILE:(t + 1) * KEY_TILE]
            tile.s[t, rows, :] = st
            mt = jnp.maximum(st[:, :LANES], st[:, LANES:])
            m = mt if m is None else jnp.maximum(m, mt)
        tile.m[rows, :] = m


def _pv_all(tile, v_ref):
    for half in range(2):
        rows = pl.ds(half * TOK_TILE, TOK_TILE)
        p = jnp.concatenate([tile.p[t, rows, :] for t in range(N_KEY_TILES)], axis=1)
        tile.acc[rows, :] = _dot(p, v_ref[...])


def _reduce_max(tile):
    tile.m[...] = jnp.broadcast_to(jnp.max(tile.m[...], axis=-1, keepdims=True), tile.m.shape)


def _exp_rows(tile, i):
    rs = pl.ds(pl.multiple_of(i * EXP_ROWS, EXP_ROWS), EXP_ROWS)
    m = tile.m[rs, :]
    parts = [None] * PARTIALS
    n = 0
    for c in range(N_KEY_TILES):
        for lo in range(0, KEY_TILE, LANES):
            p = jnp.exp2(tile.s[c, rs, lo:lo + LANES] - m)
            parts[n % PARTIALS] = p if parts[n % PARTIALS] is None else parts[n % PARTIALS] + p
            tile.p[c, rs, lo:lo + LANES] = p.astype(_BF16)
            n += 1
    tile.l[rs, :] = functools.reduce(jnp.add, parts)


def _pv_chunk(tile, t, v_ref):
    vals = v_ref[_tile_rows(t), :]
    for half in range(2):
        rows = pl.ds(half * TOK_TILE, TOK_TILE)
        tile.acc[rows, :] += _dot(tile.p[t, rows, :], vals)


def _fused_stage(scores, exps, outs, k_ref, v_ref):
    if scores is not None:
        scores.m[...] = jnp.full(scores.m.shape, -jnp.inf, _F32)
    if outs is not None:
        outs.acc[...] = jnp.zeros(outs.acc.shape, _F32)

    def trip(t, carry):
        if outs is not None:
            _pv_chunk(outs, t, v_ref)
        if exps is not None:
            _exp_rows(exps, t)
        if scores is not None:
            _qk_tile(scores, t, k_ref)
        return carry

    lax.fori_loop(0, FUSED_TRIPS, trip, 0, unroll=FUSED_UNROLL)
    if scores is not None:
        _qk_tile(scores, FUSED_TRIPS, k_ref)
        _reduce_max(scores)
    if outs is not None:
        _pv_chunk(outs, FUSED_TRIPS, v_ref)


def _attend_sequence(tiles, q_ref, k_ref, v_ref, o_ref, split, finish):
    even, odd = tiles

    def tile_rows(n):
        start = n * TOK_TILE
        return pl.ds(start if isinstance(n, int) else pl.multiple_of(start, TOK_TILE), TOK_TILE)

    def load(tile, n):
        tile.q[...] = _stack_masked(q_ref[tile_rows(n), :], split)

    def store(tile, n):
        a = tile.acc[...] / jnp.sum(tile.l[...], axis=-1, keepdims=True)
        o_ref[tile_rows(n), :] = finish(a)

    load(even, 0)
    _qk_all(even, k_ref)
    _reduce_max(even)
    load(odd, 1)
    _fused_stage(odd, even, None, k_ref, v_ref)

    def two_tiles(i, carry):
        n = 2 * i
        load(even, n + 2)
        _fused_stage(even, odd, even, k_ref, v_ref)
        store(even, n)
        load(odd, n + 3)
        _fused_stage(odd, even, odd, k_ref, v_ref)
        store(odd, n + 1)
        return carry

    lax.fori_loop(0, LATENT_TILES // 2 - 1, two_tiles, 0)
    _fused_stage(None, odd, even, k_ref, v_ref)
    store(even, LATENT_TILES - 2)
    _pv_all(odd, v_ref)
    store(odd, LATENT_TILES - 1)


def _pair_scratch(qk_width, v_width):
    one = [pltpu.VMEM((STACK_ROWS, qk_width), _BF16),
           pltpu.VMEM((N_KEY_TILES, STACK_ROWS, KEY_TILE), _F32),
           pltpu.VMEM((STACK_ROWS, LANES), _F32),
           pltpu.VMEM((N_KEY_TILES, STACK_ROWS, KEY_TILE), _BF16),
           pltpu.VMEM((STACK_ROWS, LANES), _F32),
           pltpu.VMEM((STACK_ROWS, v_width), _F32)]
    return one + one


def _pair_tiles(scratch):
    n = len(_TileScratch._fields)
    return _TileScratch(*scratch[:n]), _TileScratch(*scratch[n:])


def _attend_context(q, k_ref, v_ref):
    s = _dot(q, k_ref[...])
    p = jnp.exp2(s - jnp.max(s, axis=-1, keepdims=True))
    return _dot(p.astype(_BF16), v_ref[...]) / jnp.sum(p, axis=-1, keepdims=True)


def _dense_attention(latent_kernel, context_kernel, params, q, k, v, heads, qk_width, v_width,
                     with_context, name):
    assert KEY_TILE == TOK_TILE == CTX_LEN
    batch = q.shape[0]
    pair = 2 * TOK_TILE
    n_rows = S_ALL if with_context else SEQ
    out_shape = jax.ShapeDtypeStruct((batch, n_rows, heads * v_width), _BF16)

    param_specs = [pl.BlockSpec(p.shape, lambda *_, nd=p.ndim: (0,) * nd) for p in params]

    out = pl.pallas_call(
        latent_kernel,
        grid=(batch, heads),
        in_specs=param_specs + [
            pl.BlockSpec((None, SEQ, qk_width), lambda b, h: (b, 0, h)),
            pl.BlockSpec((None, N_KEY_TILES, qk_width, KEY_TILE), lambda b, h: (b, 0, h, 0)),
            pl.BlockSpec((None, S_ALL, v_width), lambda b, h: (b, 0, h))],
        out_specs=pl.BlockSpec((None, SEQ, v_width), lambda b, h: (b, 0, h)),
        out_shape=out_shape,
        scratch_shapes=_pair_scratch(qk_width, v_width),
        compiler_params=_params(2),
        name=name,
    )(*params, q, k, v)
    if not with_context:
        return out
    ctx_block = SEQ // CTX_LEN
    return pl.pallas_call(
        context_kernel,
        grid=(batch, heads),
        in_specs=param_specs + [
            pl.BlockSpec((None, CTX_LEN, qk_width), lambda b, h: (b, ctx_block, h)),
            pl.BlockSpec((None, None, qk_width, KEY_TILE), lambda b, h: (b, ctx_block, h, 0)),
            pl.BlockSpec((None, CTX_LEN, v_width), lambda b, h: (b, ctx_block, h)),
            pl.BlockSpec(memory_space=pl.ANY)],
        out_specs=pl.BlockSpec((None, CTX_LEN, v_width), lambda b, h: (b, ctx_block, h)),
        out_shape=out_shape,
        input_output_aliases={len(params) + 3: 0},
        compiler_params=_params(2),
        name=name + "_context",
    )(*params, q, k, v, out)


def _diff_finish(a, lam_ref, subln_ref, lam_init):
    lam = lam_ref[...]
    lam_full = (jnp.exp(jnp.sum(lam[0:1] * lam[1:2], axis=-1, keepdims=True))
                - jnp.exp(jnp.sum(lam[2:3] * lam[3:4], axis=-1, keepdims=True)) + lam_init)
    o = a[:TOK_TILE] - lam_full * a[TOK_TILE:]
    return (_rms(o) * subln_ref[...] * (1.0 - lam_init)).astype(_BF16)


def _diff_latent_kernel(lam_ref, subln_ref, q_ref, k_ref, v_ref, o_ref, *scratch, lam_init):
    finish = functools.partial(_diff_finish, lam_ref=lam_ref, subln_ref=subln_ref, lam_init=lam_init)
    _attend_sequence(_pair_tiles(scratch), q_ref, k_ref, v_ref, o_ref, B_HEAD_DIM, finish)


def _diff_context_kernel(lam_ref, subln_ref, q_ref, k_ref, v_ref, buf_ref, o_ref, *, lam_init):
    del buf_ref
    a = _attend_context(_stack_masked(q_ref[...], B_HEAD_DIM), k_ref, v_ref)
    o_ref[...] = _diff_finish(a, lam_ref, subln_ref, lam_init)


def _diff_attention(q, k, v, lam, subln, lam_init, with_context):
    head_w = 2 * B_HEAD_DIM
    return _dense_attention(functools.partial(_diff_latent_kernel, lam_init=lam_init),
                            functools.partial(_diff_context_kernel, lam_init=lam_init),
                            (lam, subln), q, k, v, B_HEADS, head_w, head_w, with_context,
                            "diff_attention")


def _mla_finish(a):
    first_v = lax.broadcasted_iota(jnp.int32, (TOK_TILE, 2 * C_V), 1) < C_V
    return jnp.where(first_v, a[:TOK_TILE], a[TOK_TILE:]).astype(_BF16)


def _mla_latent_kernel(q_ref, k_ref, v_ref, o_ref, *scratch):
    _attend_sequence(_pair_tiles(scratch), q_ref, k_ref, v_ref, o_ref, C_QK_PAD, _mla_finish)


def _mla_context_kernel(q_ref, k_ref, v_ref, buf_ref, o_ref):
    del buf_ref
    o_ref[...] = _mla_finish(_attend_context(_stack_masked(q_ref[...], C_QK_PAD), k_ref, v_ref))


def _mla_attention(q, k, v, with_context):
    return _dense_attention(_mla_latent_kernel, _mla_context_kernel, (), q, k, v, C_HEADS // 2,
                            2 * C_QK_PAD, 2 * C_V, with_context, "mla_attention")


def _post_kernel(a_ref, h_ref, mod_ref, g_ref, wo_ref, w1_ref, w2_ref, o_ref):
    y = _dot(a_ref[...], wo_ref[...])
    h1 = h_ref[...] + mod_ref[2:3, :] * (_rms(y) * g_ref[1:2, :])
    u = (_rms(h1) * g_ref[2:3, :]) * (1.0 + mod_ref[4:5, :]) + mod_ref[3:4, :]
    t = jnp.maximum(_dot(u.astype(_BF16), w1_ref[...]), 0.0)
    z = _dot((t * t).astype(_BF16), w2_ref[...])
    o_ref[...] = h1 + mod_ref[5:6, :] * (_rms(z) * g_ref[3:4, :])


def _post(a, h, mod_all, norms, w_o, w1, w2, n_tiles):
    batch = h.shape[0]
    return pl.pallas_call(
        _post_kernel,
        grid=(batch, n_tiles),
        in_specs=[_tok_spec(D_MODEL), _tok_spec(D_MODEL), _mod_spec(), _const_spec((4, D_MODEL)),
                  _const_spec((D_MODEL, D_MODEL)), _const_spec((D_MODEL, D_FF)),
                  _const_spec((D_FF, D_MODEL))],
        out_specs=_tok_spec(D_MODEL),
        out_shape=jax.ShapeDtypeStruct((batch, n_tiles * TOK_TILE, D_MODEL), _F32),
        compiler_params=_params(2),
        name="post",
    )(a, h, mod_all, norms, w_o, w1, w2)


def _a_head_perm():
    new = np.arange(A_HEADS * A_HEAD_DIM)
    r, g, d = new // A_BLOCK, (new % A_BLOCK) // A_HEAD_DIM, new % A_HEAD_DIM
    return (g * A_GROUP + r) * A_HEAD_DIM + d


def _layer_window(h, mod_all, norms, w_qkv, sink, w_o, n_tiles, tables):
    nq = A_HEADS * A_HEAD_DIM
    nk = A_KV_HEADS * A_HEAD_DIM
    perm = _a_head_perm()
    w = jnp.concatenate([w_qkv[:, :nq][:, perm], w_qkv[:, nq:]], axis=1).astype(_BF16)
    q, k, v = _qkv_proj(h, mod_all, norms, w, tables, nq, nk, nk, A_HEAD_DIM ** -0.5, False)
    a = _window_attention(q, k, v, sink, n_tiles)
    return a, w_o[perm, :]


def _layer_diff(h, mod_all, norms, w_qkv, lam, subln, w_o, layer_idx, n_tiles, tables):
    n = B_HEADS * 2 * B_HEAD_DIM
    lam_init = 0.8 - 0.6 * math.exp(-0.3 * layer_idx)
    q, k, v = _qkv_proj(h, mod_all, norms, w_qkv.astype(_BF16), tables, n, n, n,
                        B_HEAD_DIM ** -0.5 * LOG2E, True)
    a = _diff_attention(q, k, v, lam, subln.reshape(1, -1), lam_init, n_tiles == TILES_PER_SAMPLE)
    return a, w_o


def _layer_mla(h, mod_all, norms, w_in, q_norm, kv_norm, w_uq, w_ukv, w_o, n_tiles, tables):
    dqk = C_NOPE + C_ROPE
    zeros = functools.partial(jnp.zeros, dtype=_F32)
    w_in_p = jnp.concatenate(
        [w_in[:, :C_Q_LORA + C_KV_LORA], zeros((D_MODEL, C_NOPE)), w_in[:, C_Q_LORA + C_KV_LORA:],
         zeros((D_MODEL, C_QK_PAD - dqk))], axis=1).astype(_BF16)
    w_uq_p = jnp.pad(w_uq.reshape(C_Q_LORA, C_HEADS, dqk), ((0, 0), (0, 0), (0, C_QK_PAD - dqk)))
    w_uq_p = w_uq_p.reshape(C_Q_LORA, C_HEADS * C_QK_PAD).astype(_BF16)
    w_ukv_h = w_ukv.reshape(C_KV_LORA, C_HEADS, C_NOPE + C_V)
    w_uk_p = jnp.pad(w_ukv_h[:, :, :C_NOPE], ((0, 0), (0, 0), (0, C_QK_PAD - C_NOPE)))
    w_uk_p = w_uk_p.reshape(C_KV_LORA, C_HEADS * C_QK_PAD).astype(_BF16)
    w_uv = w_ukv_h[:, :, C_NOPE:].reshape(C_KV_LORA, C_HEADS * C_V).astype(_BF16)
    q, k, v = _mla_proj(h, mod_all, norms, w_in_p, q_norm.reshape(1, -1), kv_norm.reshape(1, -1),
                        w_uq_p, w_uk_p, w_uv, tables, dqk ** -0.5 * LOG2E)
    a = _mla_attention(q, k, v, n_tiles == TILES_PER_SAMPLE)
    return a, w_o


def kernel(x, c, ctx, c_ctx, l0_ada_w, l0_ada_b, l0_norms, l0_w_qkv, l0_sink, l0_w_o, l0_mlp_w1, l0_mlp_w2, l1_ada_w, l1_ada_b, l1_norms, l1_w_qkv, l1_lambda, l1_subln, l1_w_o, l1_mlp_w1, l1_mlp_w2, l2_ada_w, l2_ada_b, l2_norms, l2_w_in, l2_q_norm, l2_kv_norm, l2_w_uq, l2_w_ukv, l2_w_o, l2_mlp_w1, l2_mlp_w2, l3_ada_w, l3_ada_b, l3_norms, l3_w_qkv, l3_sink, l3_w_o, l3_mlp_w1, l3_mlp_w2):
    batch = x.shape[0]
    assert x.shape == (batch, SEQ, D_MODEL) and ctx.shape == (batch, CTX_LEN, D_MODEL) and batch < 16
    layers = [
        (l0_ada_w, l0_ada_b, l0_norms, l0_w_o, l0_mlp_w1, l0_mlp_w2),
        (l1_ada_w, l1_ada_b, l1_norms, l1_w_o, l1_mlp_w1, l1_mlp_w2),
        (l2_ada_w, l2_ada_b, l2_norms, l2_w_o, l2_mlp_w1, l2_mlp_w2),
        (l3_ada_w, l3_ada_b, l3_norms, l3_w_o, l3_mlp_w1, l3_mlp_w2),
    ]
    tables_ab = _rope_tables(A_HEAD_DIM, 0, A_HEAD_DIM)
    tables_c = _rope_tables(C_ROPE, C_NOPE, LANES)
    cc = jnp.zeros((16, D_MODEL), _F32).at[:batch].set(c).at[batch].set(c_ctx)
    h = jnp.concatenate([x, ctx], axis=1)
    depth = len(layers)
    for i, (ada_w, ada_b, norms, w_o, w1, w2) in enumerate(layers):
        n_tiles = LATENT_TILES if i == depth - 1 else TILES_PER_SAMPLE
        mod = _ada(cc, ada_w, ada_b)
        mod_lat = mod[:batch].reshape(batch, 1, 6, D_MODEL)
        mod_ctx = jnp.broadcast_to(mod[batch].reshape(1, 1, 6, D_MODEL), (batch, 1, 6, D_MODEL))
        mod_all = jnp.concatenate([mod_lat, mod_ctx], axis=1)
        if i == 0:
            a, w_o = _layer_window(h, mod_all, norms, l0_w_qkv, l0_sink, w_o, n_tiles, tables_ab)
        elif i == 1:
            a, w_o = _layer_diff(h, mod_all, norms, l1_w_qkv, l1_lambda, l1_subln, w_o, i, n_tiles,
                                 tables_ab)
        elif i == 2:
            a, w_o = _layer_mla(h, mod_all, norms, l2_w_in, l2_q_norm, l2_kv_norm, l2_w_uq, l2_w_ukv,
                                w_o, n_tiles, tables_c)
        else:
            a, w_o = _layer_window(h, mod_all, norms, l3_w_qkv, l3_sink, w_o, n_tiles, tables_ab)
        h = _post(a, h, mod_all, norms, w_o.astype(_BF16), w1.astype(_BF16), w2.astype(_BF16), n_tiles)
    return h
```

```python
import functools
import math
from typing import Any, NamedTuple

import jax
import jax.numpy as jnp
import numpy as np
from jax import lax
from jax.experimental import pallas as pl
from jax.experimental.pallas import tpu as pltpu

D_MODEL = 1024
SEQ = 4096
CTX_LEN = 256
S_ALL = SEQ + CTX_LEN
GRID_W = 64
D_FF = 4 * D_MODEL
EPS = 1e-6
ROPE_THETA = 10000.0
NEG_INF = -1e30
WINDOW = 128
LOG2E = math.log2(math.e)

A_HEADS = 16
A_KV_HEADS = 4
A_GROUP = 4
A_HEAD_DIM = 64
B_HEADS = 8
B_HEAD_DIM = 64
C_HEADS = 16
C_Q_LORA = 384
C_KV_LORA = 256
C_NOPE = 64
C_ROPE = 32
C_V = 64

LANES = 128
MXU_W = 256
TOK_TILE = 256
TILES_PER_SAMPLE = S_ALL // TOK_TILE
LATENT_TILES = SEQ // TOK_TILE
PARTIALS = 2
VMEM_LIMIT = 56 * 1024 * 1024

_BF16 = jnp.bfloat16
_F32 = jnp.float32


def _params(n_parallel):
    return pltpu.CompilerParams(
        dimension_semantics=("parallel",) * n_parallel, vmem_limit_bytes=VMEM_LIMIT)


def _rms(x):
    return x * lax.rsqrt(jnp.mean(x * x, axis=-1, keepdims=True) + EPS)


def _dot(a, b):
    return jnp.dot(a, b, preferred_element_type=_F32)


def _dot_nt(a, b):
    return lax.dot_general(a, b, (((1,), (1,)), ((), ())), preferred_element_type=_F32)


def _rope_tables(d_rot, lane_off, period):
    q = d_rot // 4
    da = d_rot // 2
    inv = ROPE_THETA ** (-jnp.arange(0, da, 2, dtype=_F32) / da)
    t = jnp.arange(SEQ, dtype=jnp.int32)
    rows = (t // GRID_W).astype(_F32)[:, None] * inv[None, :]
    cols = (t % GRID_W).astype(_F32)[:, None] * inv[None, :]
    zero = jnp.zeros((SEQ, q), _F32)
    cos = jnp.concatenate([jnp.cos(rows), jnp.cos(rows), jnp.cos(cols), jnp.cos(cols)], axis=1)
    sin_up = jnp.concatenate([-jnp.sin(rows), zero, -jnp.sin(cols), zero], axis=1)
    sin_dn = jnp.concatenate([zero, jnp.sin(rows), zero, jnp.sin(cols)], axis=1)

    def place(tab, fill):
        unit = jnp.full((SEQ, period), fill, _F32).at[:, lane_off:lane_off + d_rot].set(tab)
        full = jnp.tile(unit, (1, MXU_W // period))
        ctx = jnp.full((CTX_LEN, MXU_W), fill, _F32)
        return jnp.concatenate([full, ctx], axis=0)

    return place(cos, 1.0), place(sin_up, 0.0), place(sin_dn, 0.0)


def _rope_block(x, cos, sin_up, sin_dn, q):
    width = x.shape[1]
    up = pltpu.roll(x, width - q, 1)
    dn = pltpu.roll(x, q, 1)
    return x * cos + up * sin_up + dn * sin_dn


def _ada_kernel(c_ref, w_ref, b_ref, o_ref):
    c = c_ref[...]
    s = c * (1.0 / (1.0 + jnp.exp(-c)))
    o_ref[...] = jnp.dot(s, w_ref[...], preferred_element_type=_F32,
                         precision=lax.Precision.HIGHEST) + b_ref[...]


def _ada(cc, w, b):
    n = w.shape[1]
    bn = D_MODEL
    return pl.pallas_call(
        _ada_kernel,
        grid=(n // bn,),
        in_specs=[pl.BlockSpec((16, D_MODEL), lambda j: (0, 0)),
                  pl.BlockSpec((D_MODEL, bn), lambda j: (0, j)),
                  pl.BlockSpec((1, bn), lambda j: (0, j))],
        out_specs=pl.BlockSpec((16, bn), lambda j: (0, j)),
        out_shape=jax.ShapeDtypeStruct((16, n), _F32),
        compiler_params=_params(1),
        name="ada",
    )(cc, w, b.reshape(1, n))


def _modulated(h_ref, mod_ref, g_ref, norm_row, shift_row):
    y = _rms(h_ref[...]) * g_ref[norm_row:norm_row + 1, :]
    return y * (1.0 + mod_ref[shift_row + 1:shift_row + 2, :]) + mod_ref[shift_row:shift_row + 1, :]


def _tok_spec(width):
    return pl.BlockSpec((None, TOK_TILE, width), lambda b, j: (b, j, 0))


def _mod_spec():
    return pl.BlockSpec((None, None, 6, D_MODEL), lambda b, j: (b, j // LATENT_TILES, 0, 0))


def _const_spec(shape):
    return pl.BlockSpec(shape, lambda b, j: (0,) * len(shape))


def _table_spec():
    return pl.BlockSpec((TOK_TILE, MXU_W), lambda b, j: (j, 0))


def _keys_t_spec(width):
    return pl.BlockSpec((None, None, width, TOK_TILE), lambda b, j: (b, j, 0, 0))


def _keys_t_shape(batch, width):
    return jax.ShapeDtypeStruct((batch, TILES_PER_SAMPLE, width, TOK_TILE), _BF16)


def _qkv_kernel(h_ref, mod_ref, g_ref, w_ref, cos_ref, up_ref, dn_ref, q_ref, k_ref, v_ref,
                *, nq, nk, qscale, keys_transposed):
    u = _modulated(h_ref, mod_ref, g_ref, 0, 0).astype(_BF16)
    cos, up, dn = cos_ref[...], up_ref[...], dn_ref[...]
    quarter = A_HEAD_DIM // 4
    for blk in range(nq // MXU_W):
        x = _dot(u, w_ref[:, blk * MXU_W:(blk + 1) * MXU_W])
        x = _rope_block(x, cos, up, dn, quarter) * qscale
        q_ref[:, blk * MXU_W:(blk + 1) * MXU_W] = x.astype(_BF16)
    for blk in range(nk // MXU_W):
        x = _dot(u, w_ref[:, nq + blk * MXU_W:nq + (blk + 1) * MXU_W])
        x = _rope_block(x, cos, up, dn, quarter)
        if keys_transposed:
            k_ref[blk * MXU_W:(blk + 1) * MXU_W, :] = x.T.astype(_BF16)
        else:
            k_ref[:, blk * MXU_W:(blk + 1) * MXU_W] = x.astype(_BF16)
    v_ref[...] = _dot(u, w_ref[:, nq + nk:]).astype(_BF16)


def _qkv_proj(h, mod_all, norms, w, tables, nq, nk, nv, qscale, keys_transposed):
    batch = h.shape[0]
    bf16_rows = lambda n: jax.ShapeDtypeStruct((batch, S_ALL, n), _BF16)
    return pl.pallas_call(
        functools.partial(_qkv_kernel, nq=nq, nk=nk, qscale=qscale, keys_transposed=keys_transposed),
        grid=(batch, TILES_PER_SAMPLE),
        in_specs=[_tok_spec(D_MODEL), _mod_spec(), _const_spec((4, D_MODEL)),
                  _const_spec((D_MODEL, nq + nk + nv)),
                  _table_spec(), _table_spec(), _table_spec()],
        out_specs=[_tok_spec(nq), _keys_t_spec(nk) if keys_transposed else _tok_spec(nk), _tok_spec(nv)],
        out_shape=[bf16_rows(nq), _keys_t_shape(batch, nk) if keys_transposed else bf16_rows(nk),
                   bf16_rows(nv)],
        compiler_params=_params(2),
        name="qkv_proj",
    )(h, mod_all, norms, w, *tables)


C_QK_PAD = LANES
C_IN_PAD = C_Q_LORA + C_KV_LORA + LANES


def _mla_proj_kernel(h_ref, mod_ref, g_ref, w_in_ref, qn_ref, kvn_ref, w_uq_ref, w_uk_ref, w_uv_ref,
                     cos_ref, up_ref, dn_ref, q_ref, k_ref, v_ref, *, qscale):
    u = _modulated(h_ref, mod_ref, g_ref, 0, 0).astype(_BF16)
    cos, up, dn = cos_ref[...], up_ref[...], dn_ref[...]
    quarter = C_ROPE // 4
    t = _dot(u, w_in_ref[...])
    cq = (_rms(t[:, :C_Q_LORA]) * qn_ref[...]).astype(_BF16)
    ckv = (_rms(t[:, C_Q_LORA:C_Q_LORA + C_KV_LORA]) * kvn_ref[...]).astype(_BF16)
    kr = _rope_block(t[:, C_Q_LORA + C_KV_LORA:], cos[:, :LANES], up[:, :LANES], dn[:, :LANES], quarter)
    kr = jnp.concatenate([kr] * (MXU_W // C_QK_PAD), axis=1)
    for blk in range(C_HEADS * C_QK_PAD // MXU_W):
        sl = slice(blk * MXU_W, (blk + 1) * MXU_W)
        q = _rope_block(_dot(cq, w_uq_ref[:, sl]), cos, up, dn, quarter) * qscale
        q_ref[:, sl] = q.astype(_BF16)
        k_ref[sl, :] = (_dot(ckv, w_uk_ref[:, sl]) + kr).T.astype(_BF16)
    v_ref[...] = _dot(ckv, w_uv_ref[...]).astype(_BF16)


def _mla_proj(h, mod_all, norms, w_in, q_norm, kv_norm, w_uq, w_uk, w_uv, tables, qscale):
    batch = h.shape[0]
    nqk = C_HEADS * C_QK_PAD
    nv = C_HEADS * C_V
    return pl.pallas_call(
        functools.partial(_mla_proj_kernel, qscale=qscale),
        grid=(batch, TILES_PER_SAMPLE),
        in_specs=[_tok_spec(D_MODEL), _mod_spec(), _const_spec((4, D_MODEL)),
                  _const_spec((D_MODEL, C_IN_PAD)), _const_spec((1, C_Q_LORA)),
                  _const_spec((1, C_KV_LORA)), _const_spec((C_Q_LORA, nqk)),
                  _const_spec((C_KV_LORA, nqk)), _const_spec((C_KV_LORA, nv)),
                  _table_spec(), _table_spec(), _table_spec()],
        out_specs=[_tok_spec(nqk), _keys_t_spec(nqk), _tok_spec(nv)],
        out_shape=[jax.ShapeDtypeStruct((batch, S_ALL, nqk), _BF16), _keys_t_shape(batch, nqk),
                   jax.ShapeDtypeStruct((batch, S_ALL, nv), _BF16)],
        compiler_params=_params(2),
        name="mla_proj",
    )(h, mod_all, norms, w_in, q_norm, kv_norm, w_uq, w_uk, w_uv, *tables)


A_BLOCK = A_KV_HEADS * A_HEAD_DIM
A_KEYS = 2 * TOK_TILE + CTX_LEN


A_STACK = A_GROUP * TOK_TILE
A_KEY_TILES = A_KEYS // TOK_TILE
A_EXP_ROWS = 16


class _WindowScratch(NamedTuple):
    q: Any
    s: Any
    m: Any
    p: Any
    l: Any


def _window_kernel(sink_ref, q_ref, kp_ref, kc_ref, kn_ref, kx_ref, vp_ref, vc_ref, vn_ref, vx_ref,
                   o_ref, *scratch):
    n_fields = len(_WindowScratch._fields)
    sets = (_WindowScratch(*scratch[:n_fields]), _WindowScratch(*scratch[n_fields:]))
    j = pl.program_id(1)
    half = TOK_TILE // 2
    keys = [jnp.concatenate([kp_ref[...], kc_ref[:half, :]], axis=0),
            jnp.concatenate([kc_ref[half:, :], kn_ref[...]], axis=0), kx_ref[...]]
    vals = jnp.concatenate([vp_ref[...], vc_ref[...], vn_ref[...], vx_ref[...]], axis=0)
    qi = lax.broadcasted_iota(jnp.int32, (TOK_TILE, TOK_TILE), 0)
    col = lax.broadcasted_iota(jnp.int32, (TOK_TILE, TOK_TILE), 1)
    masks = []
    for t in range(A_KEY_TILES - 1):
        c = col + t * TOK_TILE
        kpos = TOK_TILE * j - WINDOW + c
        masks.append((c - qi >= 0) & (c - qi <= 2 * WINDOW) & (kpos >= 0) & (kpos < SEQ)
                     & (j < LATENT_TILES))
    lane_group = lax.broadcasted_iota(jnp.int32, (TOK_TILE, A_BLOCK), 1) // A_HEAD_DIM
    lane0 = lax.broadcasted_iota(jnp.int32, (A_EXP_ROWS, LANES), 1) == 0
    masked = NEG_INF * LOG2E

    def sink2(g, r):
        return sink_ref[g * A_GROUP + r] * LOG2E

    def scores(g, st):
        sel = lane_group == g
        for r in range(A_GROUP):
            rows = pl.ds(r * TOK_TILE, TOK_TILE)
            st.q[rows, :] = jnp.where(sel, q_ref[:, r * A_BLOCK:(r + 1) * A_BLOCK], jnp.zeros((), _BF16))
            st.m[rows, :] = jnp.full((TOK_TILE, LANES), sink2(g, r), _F32)
        for t in range(A_KEY_TILES):
            for r in range(A_GROUP):
                rows = pl.ds(r * TOK_TILE, TOK_TILE)
                sh = _dot_nt(st.q[rows, :], keys[t])
                if t < A_KEY_TILES - 1:
                    sh = jnp.where(masks[t], sh, masked)
                st.s[t, rows, :] = sh
                st.m[rows, :] = jnp.maximum(st.m[rows, :], jnp.maximum(sh[:, :LANES], sh[:, LANES:]))
        st.m[...] = jnp.broadcast_to(jnp.max(st.m[...], axis=-1, keepdims=True), st.m.shape)

    def exps(g, st):
        for blk in range(A_STACK // A_EXP_ROWS):
            rs = pl.ds(blk * A_EXP_ROWS, A_EXP_ROWS)
            m = st.m[rs, :]
            parts = [None] * PARTIALS
            n = 0
            for t in range(A_KEY_TILES):
                for lo in range(0, TOK_TILE, LANES):
                    p = jnp.exp2(st.s[t, rs, lo:lo + LANES] - m)
                    parts[n % PARTIALS] = p if parts[n % PARTIALS] is None else parts[n % PARTIALS] + p
                    st.p[t, rs, lo:lo + LANES] = p.astype(_BF16)
                    n += 1
            sink_term = jnp.exp2(sink2(g, blk * A_EXP_ROWS // TOK_TILE) - m)
            st.l[rs, :] = functools.reduce(jnp.add, parts) + jnp.where(lane0, sink_term, 0.0)

    def outputs(g, st):
        for hf in range(2):
            rows2 = pl.ds(hf * 2 * TOK_TILE, 2 * TOK_TILE)
            p = jnp.concatenate([st.p[t, rows2, :] for t in range(A_KEY_TILES)], axis=1)
            o = _dot(p, vals) / jnp.sum(st.l[rows2, :], axis=-1, keepdims=True)
            for hd in range(2):
                lanes = slice((2 * hf + hd) * A_BLOCK + g * A_HEAD_DIM,
                              (2 * hf + hd) * A_BLOCK + (g + 1) * A_HEAD_DIM)
                o_ref[:, lanes] = o[hd * TOK_TILE:(hd + 1) * TOK_TILE,
                                    g * A_HEAD_DIM:(g + 1) * A_HEAD_DIM].astype(_BF16)

    scores(0, sets[0])
    for g in range(A_KV_HEADS):
        if g + 1 < A_KV_HEADS:
            scores(g + 1, sets[(g + 1) % 2])
        exps(g, sets[g % 2])
        outputs(g, sets[g % 2])


def _window_attention(q, k, v, sink, n_tiles):
    batch = q.shape[0]
    half = TOK_TILE // 2
    last_half = S_ALL // half - 1

    def half_spec(fn):
        return pl.BlockSpec((None, half, A_BLOCK), fn)

    prev_spec = half_spec(lambda b, j: (b, jnp.maximum(2 * j - 1, 0), 0))
    next_spec = half_spec(lambda b, j: (b, jnp.minimum(2 * j + 2, last_half), 0))
    cur_spec = pl.BlockSpec((None, TOK_TILE, A_BLOCK), lambda b, j: (b, j, 0))
    ctx_spec = pl.BlockSpec((None, CTX_LEN, A_BLOCK), lambda b, j: (b, LATENT_TILES, 0))
    kv_specs = [prev_spec, cur_spec, next_spec, ctx_spec]
    return pl.pallas_call(
        _window_kernel,
        grid=(batch, n_tiles),
        in_specs=[pl.BlockSpec(memory_space=pltpu.SMEM), _tok_spec(D_MODEL)] + kv_specs + kv_specs,
        out_specs=_tok_spec(D_MODEL),
        out_shape=jax.ShapeDtypeStruct((batch, n_tiles * TOK_TILE, D_MODEL), _BF16),
        scratch_shapes=2 * [pltpu.VMEM((A_STACK, A_BLOCK), _BF16),
                            pltpu.VMEM((A_KEY_TILES, A_STACK, TOK_TILE), _F32),
                            pltpu.VMEM((A_STACK, LANES), _F32),
                            pltpu.VMEM((A_KEY_TILES, A_STACK, TOK_TILE), _BF16),
                            pltpu.VMEM((A_STACK, LANES), _F32)],
        compiler_params=_params(2),
        name="window_attention",
    )(sink, q, k, k, k, k, v, v, v, v)


STACK_ROWS = 2 * TOK_TILE
KEY_TILE = MXU_W
N_KEY_TILES = S_ALL // KEY_TILE
FUSED_TILES = N_KEY_TILES - 1
BODY_TILES = 8
OUT_TILES = 2
EXP_ROWS = STACK_ROWS // FUSED_TILES


def _stack_masked(q, split):
    first = lax.broadcasted_iota(jnp.int32, q.shape, 1) < split
    zero = jnp.zeros((), _BF16)
    return jnp.concatenate([jnp.where(first, q, zero), jnp.where(first, zero, q)], axis=0)


def _tile_rows(t):
    if isinstance(t, int):
        return pl.ds(t * KEY_TILE, KEY_TILE)
    return pl.ds(pl.multiple_of(t * KEY_TILE, KEY_TILE), KEY_TILE)


class _TileScratch(NamedTuple):
    q: Any
    s: Any
    m: Any
    p: Any
    l: Any
    acc: Any


def _qk_tile(tile, t, n, k_ref):
    keys_t = jnp.concatenate([k_ref[t + i] for i in range(n)], axis=1)
    for half in range(2):
        rows = pl.ds(half * TOK_TILE, TOK_TILE)
        s = _dot(tile.q[rows, :], keys_t)
        for i in range(n):
            tile.s[t + i, rows, :] = s[:, i * KEY_TILE:(i + 1) * KEY_TILE]
        lane_blocks = [s[:, lo:lo + LANES] for lo in range(0, n * KEY_TILE, LANES)]
        tile.m[rows, :] = jnp.maximum(tile.m[rows, :], functools.reduce(jnp.maximum, lane_blocks))


def _qk_all(tile, k_ref):
    keys_t = jnp.concatenate([k_ref[t] for t in range(N_KEY_TILES)], axis=1)
    for half in range(2):
        rows = pl.ds(half * TOK_TILE, TOK_TILE)
        s = _dot(tile.q[rows, :], keys_t)
        m = None
        for t in range(N_KEY_TILES):
            st = s[:, t * KEY_TILE:(t + 1) * KEY_TILE]
            tile.s[t, rows, :] = st
            mt = jnp.maximum(st[:, :LANES], st[:, LANES:])
            m = mt if m is None else jnp.maximum(m, mt)
        tile.m[rows, :] = m


def _pv_all(tile, v_ref):
    for half in range(2):
        rows = pl.ds(half * TOK_TILE, TOK_TILE)
        p = jnp.concatenate([tile.p[t, rows, :] for t in range(N_KEY_TILES)], axis=1)
        tile.acc[rows, :] = _dot(p, v_ref[...])


def _reduce_max(tile):
    tile.m[...] = jnp.broadcast_to(jnp.max(tile.m[...], axis=-1, keepdims=True), tile.m.shape)


def _exp_rows(tile, i):
    rs = pl.ds(pl.multiple_of(i * EXP_ROWS, EXP_ROWS), EXP_ROWS)
    m = tile.m[rs, :]
    parts = [None] * PARTIALS
    n = 0
    for c in range(N_KEY_TILES):
        for lo in range(0, KEY_TILE, LANES):
            p = jnp.exp2(tile.s[c, rs, lo:lo + LANES] - m)
            parts[n % PARTIALS] = p if parts[n % PARTIALS] is None else parts[n % PARTIALS] + p
            tile.p[c, rs, lo:lo + LANES] = p.astype(_BF16)
            n += 1
    tile.l[rs, :] = functools.reduce(jnp.add, parts)


def _pv_chunk(tile, t, n, v_ref):
    start = t * KEY_TILE
    if not isinstance(t, int):
        start = pl.multiple_of(start, KEY_TILE)
    vals = v_ref[pl.ds(start, n * KEY_TILE), :]
    for half in range(2):
        rows = pl.ds(half * TOK_TILE, TOK_TILE)
        p = jnp.concatenate([tile.p[t + i, rows, :] for i in range(n)], axis=1)
        tile.acc[rows, :] += _dot(p, vals)


def _fused_stage(scores, exps, outs, k_ref, v_ref):
    if scores is not None:
        scores.m[...] = jnp.full(scores.m.shape, -jnp.inf, _F32)
    if outs is not None:
        outs.acc[...] = jnp.zeros(outs.acc.shape, _F32)

    def body(i, carry):
        base = i * BODY_TILES
        for u in range(BODY_TILES):
            if outs is not None and u % OUT_TILES == 0:
                _pv_chunk(outs, base + u, OUT_TILES, v_ref)
            if exps is not None:
                _exp_rows(exps, base + u)
            if scores is not None:
                _qk_tile(scores, base + u, 1, k_ref)
        return carry

    lax.fori_loop(0, FUSED_TILES // BODY_TILES, body, 0)
    for t in range(FUSED_TILES, N_KEY_TILES):
        if scores is not None:
            _qk_tile(scores, t, 1, k_ref)
        if outs is not None:
            _pv_chunk(outs, t, 1, v_ref)
    if scores is not None:
        _reduce_max(scores)


def _attend_sequence(tiles, q_ref, k_ref, v_ref, o_ref, split, finish):
    even, odd = tiles

    def tile_rows(n):
        start = n * TOK_TILE
        return pl.ds(start if isinstance(n, int) else pl.multiple_of(start, TOK_TILE), TOK_TILE)

    def load(tile, n):
        tile.q[...] = _stack_masked(q_ref[tile_rows(n), :], split)

    def store(tile, n):
        a = tile.acc[...] / jnp.sum(tile.l[...], axis=-1, keepdims=True)
        o_ref[tile_rows(n), :] = finish(a)

    load(even, 0)
    _qk_all(even, k_ref)
    _reduce_max(even)
    load(odd, 1)
    _fused_stage(odd, even, None, k_ref, v_ref)

    def two_tiles(i, carry):
        n = 2 * i
        load(even, n + 2)
        _fused_stage(even, odd, even, k_ref, v_ref)
        store(even, n)
        load(odd, n + 3)
        _fused_stage(odd, even, odd, k_ref, v_ref)
        store(odd, n + 1)
        return carry

    lax.fori_loop(0, LATENT_TILES // 2 - 1, two_tiles, 0)
    _fused_stage(None, odd, even, k_ref, v_ref)
    store(even, LATENT_TILES - 2)
    _pv_all(odd, v_ref)
    store(odd, LATENT_TILES - 1)


def _pair_scratch(qk_width, v_width):
    one = [pltpu.VMEM((STACK_ROWS, qk_width), _BF16),
           pltpu.VMEM((N_KEY_TILES, STACK_ROWS, KEY_TILE), _F32),
           pltpu.VMEM((STACK_ROWS, LANES), _F32),
           pltpu.VMEM((N_KEY_TILES, STACK_ROWS, KEY_TILE), _BF16),
           pltpu.VMEM((STACK_ROWS, LANES), _F32),
           pltpu.VMEM((STACK_ROWS, v_width), _F32)]
    return one + one


def _pair_tiles(scratch):
    n = len(_TileScratch._fields)
    return _TileScratch(*scratch[:n]), _TileScratch(*scratch[n:])


def _attend_context(q, k_ref, v_ref):
    s = _dot(q, k_ref[...])
    p = jnp.exp2(s - jnp.max(s, axis=-1, keepdims=True))
    return _dot(p.astype(_BF16), v_ref[...]) / jnp.sum(p, axis=-1, keepdims=True)


def _dense_attention(latent_kernel, context_kernel, params, q, k, v, heads, qk_width, v_width,
                     with_context, name):
    assert KEY_TILE == TOK_TILE == CTX_LEN
    batch = q.shape[0]
    pair = 2 * TOK_TILE
    n_rows = S_ALL if with_context else SEQ
    out_shape = jax.ShapeDtypeStruct((batch, n_rows, heads * v_width), _BF16)

    param_specs = [pl.BlockSpec(p.shape, lambda *_, nd=p.ndim: (0,) * nd) for p in params]

    out = pl.pallas_call(
        latent_kernel,
        grid=(batch, heads),
        in_specs=param_specs + [
            pl.BlockSpec((None, SEQ, qk_width), lambda b, h: (b, 0, h)),
            pl.BlockSpec((None, N_KEY_TILES, qk_width, KEY_TILE), lambda b, h: (b, 0, h, 0)),
            pl.BlockSpec((None, S_ALL, v_width), lambda b, h: (b, 0, h))],
        out_specs=pl.BlockSpec((None, SEQ, v_width), lambda b, h: (b, 0, h)),
        out_shape=out_shape,
        scratch_shapes=_pair_scratch(qk_width, v_width),
        compiler_params=_params(2),
        name=name,
    )(*params, q, k, v)
    if not with_context:
        return out
    ctx_block = SEQ // CTX_LEN
    return pl.pallas_call(
        context_kernel,
        grid=(batch, heads),
        in_specs=param_specs + [
            pl.BlockSpec((None, CTX_LEN, qk_width), lambda b, h: (b, ctx_block, h)),
            pl.BlockSpec((None, None, qk_width, KEY_TILE), lambda b, h: (b, ctx_block, h, 0)),
            pl.BlockSpec((None, CTX_LEN, v_width), lambda b, h: (b, ctx_block, h)),
            pl.BlockSpec(memory_space=pl.ANY)],
        out_specs=pl.BlockSpec((None, CTX_LEN, v_width), lambda b, h: (b, ctx_block, h)),
        out_shape=out_shape,
        input_output_aliases={len(params) + 3: 0},
        compiler_params=_params(2),
        name=name + "_context",
    )(*params, q, k, v, out)


def _diff_finish(a, lam_ref, subln_ref, lam_init):
    lam = lam_ref[...]
    lam_full = (jnp.exp(jnp.sum(lam[0:1] * lam[1:2], axis=-1, keepdims=True))
                - jnp.exp(jnp.sum(lam[2:3] * lam[3:4], axis=-1, keepdims=True)) + lam_init)
    o = a[:TOK_TILE] - lam_full * a[TOK_TILE:]
    return (_rms(o) * subln_ref[...] * (1.0 - lam_init)).astype(_BF16)


def _diff_latent_kernel(lam_ref, subln_ref, q_ref, k_ref, v_ref, o_ref, *scratch, lam_init):
    finish = functools.partial(_diff_finish, lam_ref=lam_ref, subln_ref=subln_ref, lam_init=lam_init)
    _attend_sequence(_pair_tiles(scratch), q_ref, k_ref, v_ref, o_ref, B_HEAD_DIM, finish)


def _diff_context_kernel(lam_ref, subln_ref, q_ref, k_ref, v_ref, buf_ref, o_ref, *, lam_init):
    del buf_ref
    a = _attend_context(_stack_masked(q_ref[...], B_HEAD_DIM), k_ref, v_ref)
    o_ref[...] = _diff_finish(a, lam_ref, subln_ref, lam_init)


def _diff_attention(q, k, v, lam, subln, lam_init, with_context):
    head_w = 2 * B_HEAD_DIM
    return _dense_attention(functools.partial(_diff_latent_kernel, lam_init=lam_init),
                            functools.partial(_diff_context_kernel, lam_init=lam_init),
                            (lam, subln), q, k, v, B_HEADS, head_w, head_w, with_context,
                            "diff_attention")


def _mla_finish(a):
    first_v = lax.broadcasted_iota(jnp.int32, (TOK_TILE, 2 * C_V), 1) < C_V
    return jnp.where(first_v, a[:TOK_TILE], a[TOK_TILE:]).astype(_BF16)


def _mla_latent_kernel(q_ref, k_ref, v_ref, o_ref, *scratch):
    _attend_sequence(_pair_tiles(scratch), q_ref, k_ref, v_ref, o_ref, C_QK_PAD, _mla_finish)


def _mla_context_kernel(q_ref, k_ref, v_ref, buf_ref, o_ref):
    del buf_ref
    o_ref[...] = _mla_finish(_attend_context(_stack_masked(q_ref[...], C_QK_PAD), k_ref, v_ref))


def _mla_attention(q, k, v, with_context):
    return _dense_attention(_mla_latent_kernel, _mla_context_kernel, (), q, k, v, C_HEADS // 2,
                            2 * C_QK_PAD, 2 * C_V, with_context, "mla_attention")


def _post_kernel(a_ref, h_ref, mod_ref, g_ref, wo_ref, w1_ref, w2_ref, o_ref):
    y = _dot(a_ref[...], wo_ref[...])
    h1 = h_ref[...] + mod_ref[2:3, :] * (_rms(y) * g_ref[1:2, :])
    u = (_rms(h1) * g_ref[2:3, :]) * (1.0 + mod_ref[4:5, :]) + mod_ref[3:4, :]
    t = jnp.maximum(_dot(u.astype(_BF16), w1_ref[...]), 0.0)
    z = _dot((t * t).astype(_BF16), w2_ref[...])
    o_ref[...] = h1 + mod_ref[5:6, :] * (_rms(z) * g_ref[3:4, :])


def _post(a, h, mod_all, norms, w_o, w1, w2, n_tiles):
    batch = h.shape[0]
    return pl.pallas_call(
        _post_kernel,
        grid=(batch, n_tiles),
        in_specs=[_tok_spec(D_MODEL), _tok_spec(D_MODEL), _mod_spec(), _const_spec((4, D_MODEL)),
                  _const_spec((D_MODEL, D_MODEL)), _const_spec((D_MODEL, D_FF)),
                  _const_spec((D_FF, D_MODEL))],
        out_specs=_tok_spec(D_MODEL),
        out_shape=jax.ShapeDtypeStruct((batch, n_tiles * TOK_TILE, D_MODEL), _F32),
        compiler_params=_params(2),
        name="post",
    )(a, h, mod_all, norms, w_o, w1, w2)


def _a_head_perm():
    new = np.arange(A_HEADS * A_HEAD_DIM)
    r, g, d = new // A_BLOCK, (new % A_BLOCK) // A_HEAD_DIM, new % A_HEAD_DIM
    return (g * A_GROUP + r) * A_HEAD_DIM + d


def _layer_window(h, mod_all, norms, w_qkv, sink, w_o, n_tiles, tables):
    nq = A_HEADS * A_HEAD_DIM
    nk = A_KV_HEADS * A_HEAD_DIM
    perm = _a_head_perm()
    w = jnp.concatenate([w_qkv[:, :nq][:, perm], w_qkv[:, nq:]], axis=1).astype(_BF16)
    q, k, v = _qkv_proj(h, mod_all, norms, w, tables, nq, nk, nk, A_HEAD_DIM ** -0.5 * LOG2E, False)
    a = _window_attention(q, k, v, sink, n_tiles)
    return a, w_o[perm, :]


def _layer_diff(h, mod_all, norms, w_qkv, lam, subln, w_o, layer_idx, n_tiles, tables):
    n = B_HEADS * 2 * B_HEAD_DIM
    lam_init = 0.8 - 0.6 * math.exp(-0.3 * layer_idx)
    q, k, v = _qkv_proj(h, mod_all, norms, w_qkv.astype(_BF16), tables, n, n, n,
                        B_HEAD_DIM ** -0.5 * LOG2E, True)
    a = _diff_attention(q, k, v, lam, subln.reshape(1, -1), lam_init, n_tiles == TILES_PER_SAMPLE)
    return a, w_o


def _layer_mla(h, mod_all, norms, w_in, q_norm, kv_norm, w_uq, w_ukv, w_o, n_tiles, tables):
    dqk = C_NOPE + C_ROPE
    zeros = functools.partial(jnp.zeros, dtype=_F32)
    w_in_p = jnp.concatenate(
        [w_in[:, :C_Q_LORA + C_KV_LORA], zeros((D_MODEL, C_NOPE)), w_in[:, C_Q_LORA + C_KV_LORA:],
         zeros((D_MODEL, C_QK_PAD - dqk))], axis=1).astype(_BF16)
    w_uq_p = jnp.pad(w_uq.reshape(C_Q_LORA, C_HEADS, dqk), ((0, 0), (0, 0), (0, C_QK_PAD - dqk)))
    w_uq_p = w_uq_p.reshape(C_Q_LORA, C_HEADS * C_QK_PAD).astype(_BF16)
    w_ukv_h = w_ukv.reshape(C_KV_LORA, C_HEADS, C_NOPE + C_V)
    w_uk_p = jnp.pad(w_ukv_h[:, :, :C_NOPE], ((0, 0), (0, 0), (0, C_QK_PAD - C_NOPE)))
    w_uk_p = w_uk_p.reshape(C_KV_LORA, C_HEADS * C_QK_PAD).astype(_BF16)
    w_uv = w_ukv_h[:, :, C_NOPE:].reshape(C_KV_LORA, C_HEADS * C_V).astype(_BF16)
    q, k, v = _mla_proj(h, mod_all, norms, w_in_p, q_norm.reshape(1, -1), kv_norm.reshape(1, -1),
                        w_uq_p, w_uk_p, w_uv, tables, dqk ** -0.5 * LOG2E)
    a = _mla_attention(q, k, v, n_tiles == TILES_PER_SAMPLE)
    return a, w_o


def kernel(x, c, ctx, c_ctx, l0_ada_w, l0_ada_b, l0_norms, l0_w_qkv, l0_sink, l0_w_o, l0_mlp_w1, l0_mlp_w2, l1_ada_w, l1_ada_b, l1_norms, l1_w_qkv, l1_lambda, l1_subln, l1_w_o, l1_mlp_w1, l1_mlp_w2, l2_ada_w, l2_ada_b, l2_norms, l2_w_in, l2_q_norm, l2_kv_norm, l2_w_uq, l2_w_ukv, l2_w_o, l2_mlp_w1, l2_mlp_w2, l3_ada_w, l3_ada_b, l3_norms, l3_w_qkv, l3_sink, l3_w_o, l3_mlp_w1, l3_mlp_w2):
    batch = x.shape[0]
    assert x.shape == (batch, SEQ, D_MODEL) and ctx.shape == (batch, CTX_LEN, D_MODEL) and batch < 16
    layers = [
        (l0_ada_w, l0_ada_b, l0_norms, l0_w_o, l0_mlp_w1, l0_mlp_w2),
        (l1_ada_w, l1_ada_b, l1_norms, l1_w_o, l1_mlp_w1, l1_mlp_w2),
        (l2_ada_w, l2_ada_b, l2_norms, l2_w_o, l2_mlp_w1, l2_mlp_w2),
        (l3_ada_w, l3_ada_b, l3_norms, l3_w_o, l3_mlp_w1, l3_mlp_w2),
    ]
    tables_ab = _rope_tables(A_HEAD_DIM, 0, A_HEAD_DIM)
    tables_c = _rope_tables(C_ROPE, C_NOPE, LANES)
    cc = jnp.zeros((16, D_MODEL), _F32).at[:batch].set(c).at[batch].set(c_ctx)
    h = jnp.concatenate([x, ctx], axis=1)
    depth = len(layers)
    for i, (ada_w, ada_b, norms, w_o, w1, w2) in enumerate(layers):
        n_tiles = LATENT_TILES if i == depth - 1 else TILES_PER_SAMPLE
        mod = _ada(cc, ada_w, ada_b)
        mod_lat = mod[:batch].reshape(batch, 1, 6, D_MODEL)
        mod_ctx = jnp.broadcast_to(mod[batch].reshape(1, 1, 6, D_MODEL), (batch, 1, 6, D_MODEL))
        mod_all = jnp.concatenate([mod_lat, mod_ctx], axis=1)
        if i == 0:
            a, w_o = _layer_window(h, mod_all, norms, l0_w_qkv, l0_sink, w_o, n_tiles, tables_ab)
        elif i == 1:
            a, w_o = _layer_diff(h, mod_all, norms, l1_w_qkv, l1_lambda, l1_subln, w_o, i, n_tiles,
                                 tables_ab)
        elif i == 2:
            a, w_o = _layer_mla(h, mod_all, norms, l2_w_in, l2_q_norm, l2_kv_norm, l2_w_uq, l2_w_ukv,
                                w_o, n_tiles, tables_c)
        else:
            a, w_o = _layer_window(h, mod_all, norms, l3_w_qkv, l3_sink, w_o, n_tiles, tables_ab)
        h = _post(a, h, mod_all, norms, w_o.astype(_BF16), w1.astype(_BF16), w2.astype(_BF16), n_tiles)
    return h
```

```python
import functools
import math
from typing import Any, NamedTuple

import jax
import jax.numpy as jnp
import numpy as np
from jax import lax
from jax.experimental import pallas as pl
from jax.experimental.pallas import tpu as pltpu

D_MODEL = 1024
SEQ = 4096
CTX_LEN = 256
S_ALL = SEQ + CTX_LEN
GRID_W = 64
D_FF = 4 * D_MODEL
EPS = 1e-6
ROPE_THETA = 10000.0
NEG_INF = -1e30
WINDOW = 128
LOG2E = math.log2(math.e)

A_HEADS = 16
A_KV_HEADS = 4
A_GROUP = 4
A_HEAD_DIM = 64
B_HEADS = 8
B_HEAD_DIM = 64
C_HEADS = 16
C_Q_LORA = 384
C_KV_LORA = 256
C_NOPE = 64
C_ROPE = 32
C_V = 64

LANES = 128
MXU_W = 256
TOK_TILE = 256
TILES_PER_SAMPLE = S_ALL // TOK_TILE
LATENT_TILES = SEQ // TOK_TILE
PARTIALS = 2
VMEM_LIMIT = 56 * 1024 * 1024

_BF16 = jnp.bfloat16
_F32 = jnp.float32


def _params(n_parallel):
    return pltpu.CompilerParams(
        dimension_semantics=("parallel",) * n_parallel, vmem_limit_bytes=VMEM_LIMIT)


def _rms(x):
    return x * lax.rsqrt(jnp.mean(x * x, axis=-1, keepdims=True) + EPS)


def _dot(a, b):
    return jnp.dot(a, b, preferred_element_type=_F32)


def _dot_nt(a, b):
    return lax.dot_general(a, b, (((1,), (1,)), ((), ())), preferred_element_type=_F32)


def _rope_tables(d_rot, lane_off, period):
    q = d_rot // 4
    da = d_rot // 2
    inv = ROPE_THETA ** (-jnp.arange(0, da, 2, dtype=_F32) / da)
    t = jnp.arange(SEQ, dtype=jnp.int32)
    rows = (t // GRID_W).astype(_F32)[:, None] * inv[None, :]
    cols = (t % GRID_W).astype(_F32)[:, None] * inv[None, :]
    zero = jnp.zeros((SEQ, q), _F32)
    cos = jnp.concatenate([jnp.cos(rows), jnp.cos(rows), jnp.cos(cols), jnp.cos(cols)], axis=1)
    sin_up = jnp.concatenate([-jnp.sin(rows), zero, -jnp.sin(cols), zero], axis=1)
    sin_dn = jnp.concatenate([zero, jnp.sin(rows), zero, jnp.sin(cols)], axis=1)

    def place(tab, fill):
        unit = jnp.full((SEQ, period), fill, _F32).at[:, lane_off:lane_off + d_rot].set(tab)
        full = jnp.tile(unit, (1, MXU_W // period))
        ctx = jnp.full((CTX_LEN, MXU_W), fill, _F32)
        return jnp.concatenate([full, ctx], axis=0)

    return place(cos, 1.0), place(sin_up, 0.0), place(sin_dn, 0.0)


def _rope_block(x, cos, sin_up, sin_dn, q):
    width = x.shape[1]
    up = pltpu.roll(x, width - q, 1)
    dn = pltpu.roll(x, q, 1)
    return x * cos + up * sin_up + dn * sin_dn


def _ada_kernel(c_ref, w_ref, b_ref, o_ref):
    c = c_ref[...]
    s = c * (1.0 / (1.0 + jnp.exp(-c)))
    o_ref[...] = jnp.dot(s, w_ref[...], preferred_element_type=_F32,
                         precision=lax.Precision.HIGHEST) + b_ref[...]


def _ada(cc, w, b):
    n = w.shape[1]
    bn = D_MODEL
    return pl.pallas_call(
        _ada_kernel,
        grid=(n // bn,),
        in_specs=[pl.BlockSpec((16, D_MODEL), lambda j: (0, 0)),
                  pl.BlockSpec((D_MODEL, bn), lambda j: (0, j)),
                  pl.BlockSpec((1, bn), lambda j: (0, j))],
        out_specs=pl.BlockSpec((16, bn), lambda j: (0, j)),
        out_shape=jax.ShapeDtypeStruct((16, n), _F32),
        compiler_params=_params(1),
        name="ada",
    )(cc, w, b.reshape(1, n))


def _modulated(h_ref, mod_ref, g_ref, norm_row, shift_row):
    y = _rms(h_ref[...]) * g_ref[norm_row:norm_row + 1, :]
    return y * (1.0 + mod_ref[shift_row + 1:shift_row + 2, :]) + mod_ref[shift_row:shift_row + 1, :]


def _tok_spec(width):
    return pl.BlockSpec((None, TOK_TILE, width), lambda b, j: (b, j, 0))


def _mod_spec():
    return pl.BlockSpec((None, None, 6, D_MODEL), lambda b, j: (b, j // LATENT_TILES, 0, 0))


def _const_spec(shape):
    return pl.BlockSpec(shape, lambda b, j: (0,) * len(shape))


def _table_spec():
    return pl.BlockSpec((TOK_TILE, MXU_W), lambda b, j: (j, 0))


def _keys_t_spec(width):
    return pl.BlockSpec((None, None, width, TOK_TILE), lambda b, j: (b, j, 0, 0))


def _keys_t_shape(batch, width):
    return jax.ShapeDtypeStruct((batch, TILES_PER_SAMPLE, width, TOK_TILE), _BF16)


def _qkv_kernel(h_ref, mod_ref, g_ref, w_ref, cos_ref, up_ref, dn_ref, q_ref, k_ref, v_ref,
                *, nq, nk, qscale, keys_transposed):
    u = _modulated(h_ref, mod_ref, g_ref, 0, 0).astype(_BF16)
    cos, up, dn = cos_ref[...], up_ref[...], dn_ref[...]
    quarter = A_HEAD_DIM // 4
    for blk in range(nq // MXU_W):
        x = _dot(u, w_ref[:, blk * MXU_W:(blk + 1) * MXU_W])
        x = _rope_block(x, cos, up, dn, quarter) * qscale
        q_ref[:, blk * MXU_W:(blk + 1) * MXU_W] = x.astype(_BF16)
    for blk in range(nk // MXU_W):
        x = _dot(u, w_ref[:, nq + blk * MXU_W:nq + (blk + 1) * MXU_W])
        x = _rope_block(x, cos, up, dn, quarter)
        if keys_transposed:
            k_ref[blk * MXU_W:(blk + 1) * MXU_W, :] = x.T.astype(_BF16)
        else:
            k_ref[:, blk * MXU_W:(blk + 1) * MXU_W] = x.astype(_BF16)
    v_ref[...] = _dot(u, w_ref[:, nq + nk:]).astype(_BF16)


def _qkv_proj(h, mod_all, norms, w, tables, nq, nk, nv, qscale, keys_transposed):
    batch = h.shape[0]
    bf16_rows = lambda n: jax.ShapeDtypeStruct((batch, S_ALL, n), _BF16)
    return pl.pallas_call(
        functools.partial(_qkv_kernel, nq=nq, nk=nk, qscale=qscale, keys_transposed=keys_transposed),
        grid=(batch, TILES_PER_SAMPLE),
        in_specs=[_tok_spec(D_MODEL), _mod_spec(), _const_spec((4, D_MODEL)),
                  _const_spec((D_MODEL, nq + nk + nv)),
                  _table_spec(), _table_spec(), _table_spec()],
        out_specs=[_tok_spec(nq), _keys_t_spec(nk) if keys_transposed else _tok_spec(nk), _tok_spec(nv)],
        out_shape=[bf16_rows(nq), _keys_t_shape(batch, nk) if keys_transposed else bf16_rows(nk),
                   bf16_rows(nv)],
        compiler_params=_params(2),
        name="qkv_proj",
    )(h, mod_all, norms, w, *tables)


C_QK_PAD = LANES
C_IN_PAD = C_Q_LORA + C_KV_LORA + LANES


def _mla_proj_kernel(h_ref, mod_ref, g_ref, w_in_ref, qn_ref, kvn_ref, w_uq_ref, w_uk_ref, w_uv_ref,
                     cos_ref, up_ref, dn_ref, q_ref, k_ref, v_ref, *, qscale):
    u = _modulated(h_ref, mod_ref, g_ref, 0, 0).astype(_BF16)
    cos, up, dn = cos_ref[...], up_ref[...], dn_ref[...]
    quarter = C_ROPE // 4
    t = _dot(u, w_in_ref[...])
    cq = (_rms(t[:, :C_Q_LORA]) * qn_ref[...]).astype(_BF16)
    ckv = (_rms(t[:, C_Q_LORA:C_Q_LORA + C_KV_LORA]) * kvn_ref[...]).astype(_BF16)
    kr = _rope_block(t[:, C_Q_LORA + C_KV_LORA:], cos[:, :LANES], up[:, :LANES], dn[:, :LANES], quarter)
    kr = jnp.concatenate([kr] * (MXU_W // C_QK_PAD), axis=1)
    for blk in range(C_HEADS * C_QK_PAD // MXU_W):
        sl = slice(blk * MXU_W, (blk + 1) * MXU_W)
        q = _rope_block(_dot(cq, w_uq_ref[:, sl]), cos, up, dn, quarter) * qscale
        q_ref[:, sl] = q.astype(_BF16)
        k_ref[sl, :] = (_dot(ckv, w_uk_ref[:, sl]) + kr).T.astype(_BF16)
    v_ref[...] = _dot(ckv, w_uv_ref[...]).astype(_BF16)


def _mla_proj(h, mod_all, norms, w_in, q_norm, kv_norm, w_uq, w_uk, w_uv, tables, qscale):
    batch = h.shape[0]
    nqk = C_HEADS * C_QK_PAD
    nv = C_HEADS * C_V
    return pl.pallas_call(
        functools.partial(_mla_proj_kernel, qscale=qscale),
        grid=(batch, TILES_PER_SAMPLE),
        in_specs=[_tok_spec(D_MODEL), _mod_spec(), _const_spec((4, D_MODEL)),
                  _const_spec((D_MODEL, C_IN_PAD)), _const_spec((1, C_Q_LORA)),
                  _const_spec((1, C_KV_LORA)), _const_spec((C_Q_LORA, nqk)),
                  _const_spec((C_KV_LORA, nqk)), _const_spec((C_KV_LORA, nv)),
                  _table_spec(), _table_spec(), _table_spec()],
        out_specs=[_tok_spec(nqk), _keys_t_spec(nqk), _tok_spec(nv)],
        out_shape=[jax.ShapeDtypeStruct((batch, S_ALL, nqk), _BF16), _keys_t_shape(batch, nqk),
                   jax.ShapeDtypeStruct((batch, S_ALL, nv), _BF16)],
        compiler_params=_params(2),
        name="mla_proj",
    )(h, mod_all, norms, w_in, q_norm, kv_norm, w_uq, w_uk, w_uv, *tables)


A_BLOCK = A_KV_HEADS * A_HEAD_DIM
A_KEYS = 2 * TOK_TILE + CTX_LEN


A_STACK = A_GROUP * TOK_TILE
A_KEY_TILES = A_KEYS // TOK_TILE
A_EXP_ROWS = 16


class _WindowScratch(NamedTuple):
    q: Any
    s: Any
    m: Any
    p: Any
    l: Any


def _window_kernel(sink_ref, q_ref, kp_ref, kc_ref, kn_ref, kx_ref, vp_ref, vc_ref, vn_ref, vx_ref,
                   o_ref, *scratch):
    n_fields = len(_WindowScratch._fields)
    sets = (_WindowScratch(*scratch[:n_fields]), _WindowScratch(*scratch[n_fields:]))
    j = pl.program_id(1)
    half = TOK_TILE // 2
    keys = [jnp.concatenate([kp_ref[...], kc_ref[:half, :]], axis=0),
            jnp.concatenate([kc_ref[half:, :], kn_ref[...]], axis=0), kx_ref[...]]
    vals = jnp.concatenate([vp_ref[...], vc_ref[...], vn_ref[...], vx_ref[...]], axis=0)
    qi = lax.broadcasted_iota(jnp.int32, (TOK_TILE, TOK_TILE), 0)
    col = lax.broadcasted_iota(jnp.int32, (TOK_TILE, TOK_TILE), 1)
    masks = []
    for t in range(A_KEY_TILES - 1):
        c = col + t * TOK_TILE
        kpos = TOK_TILE * j - WINDOW + c
        masks.append((c - qi >= 0) & (c - qi <= 2 * WINDOW) & (kpos >= 0) & (kpos < SEQ)
                     & (j < LATENT_TILES))
    lane_group = lax.broadcasted_iota(jnp.int32, (TOK_TILE, A_BLOCK), 1) // A_HEAD_DIM
    lane0 = lax.broadcasted_iota(jnp.int32, (A_EXP_ROWS, LANES), 1) == 0
    masked = NEG_INF * LOG2E

    def sink2(g, r):
        return sink_ref[g * A_GROUP + r] * LOG2E

    def scores(g, st):
        sel = lane_group == g
        for r in range(A_GROUP):
            rows = pl.ds(r * TOK_TILE, TOK_TILE)
            st.q[rows, :] = jnp.where(sel, q_ref[:, r * A_BLOCK:(r + 1) * A_BLOCK], jnp.zeros((), _BF16))
            st.m[rows, :] = jnp.full((TOK_TILE, LANES), sink2(g, r), _F32)
        for t in range(A_KEY_TILES):
            for r in range(A_GROUP):
                rows = pl.ds(r * TOK_TILE, TOK_TILE)
                sh = _dot_nt(st.q[rows, :], keys[t])
                if t < A_KEY_TILES - 1:
                    sh = jnp.where(masks[t], sh, masked)
                st.s[t, rows, :] = sh
                st.m[rows, :] = jnp.maximum(st.m[rows, :], jnp.maximum(sh[:, :LANES], sh[:, LANES:]))
        st.m[...] = jnp.broadcast_to(jnp.max(st.m[...], axis=-1, keepdims=True), st.m.shape)

    def exps(g, st):
        for blk in range(A_STACK // A_EXP_ROWS):
            rs = pl.ds(blk * A_EXP_ROWS, A_EXP_ROWS)
            m = st.m[rs, :]
            parts = [None] * PARTIALS
            n = 0
            for t in range(A_KEY_TILES):
                for lo in range(0, TOK_TILE, LANES):
                    p = jnp.exp2(st.s[t, rs, lo:lo + LANES] - m)
                    parts[n % PARTIALS] = p if parts[n % PARTIALS] is None else parts[n % PARTIALS] + p
                    st.p[t, rs, lo:lo + LANES] = p.astype(_BF16)
                    n += 1
            sink_term = jnp.exp2(sink2(g, blk * A_EXP_ROWS // TOK_TILE) - m)
            st.l[rs, :] = functools.reduce(jnp.add, parts) + jnp.where(lane0, sink_term, 0.0)

    def outputs(g, st):
        for hf in range(2):
            rows2 = pl.ds(hf * 2 * TOK_TILE, 2 * TOK_TILE)
            p = jnp.concatenate([st.p[t, rows2, :] for t in range(A_KEY_TILES)], axis=1)
            o = _dot(p, vals) / jnp.sum(st.l[rows2, :], axis=-1, keepdims=True)
            for hd in range(2):
                lanes = slice((2 * hf + hd) * A_BLOCK + g * A_HEAD_DIM,
                              (2 * hf + hd) * A_BLOCK + (g + 1) * A_HEAD_DIM)
                o_ref[:, lanes] = o[hd * TOK_TILE:(hd + 1) * TOK_TILE,
                                    g * A_HEAD_DIM:(g + 1) * A_HEAD_DIM].astype(_BF16)

    scores(0, sets[0])
    for g in range(A_KV_HEADS):
        if g + 1 < A_KV_HEADS:
            scores(g + 1, sets[(g + 1) % 2])
        exps(g, sets[g % 2])
        outputs(g, sets[g % 2])


def _window_attention(q, k, v, sink, n_tiles):
    batch = q.shape[0]
    half = TOK_TILE // 2
    last_half = S_ALL // half - 1

    def half_spec(fn):
        return pl.BlockSpec((None, half, A_BLOCK), fn)

    prev_spec = half_spec(lambda b, j: (b, jnp.maximum(2 * j - 1, 0), 0))
    next_spec = half_spec(lambda b, j: (b, jnp.minimum(2 * j + 2, last_half), 0))
    cur_spec = pl.BlockSpec((None, TOK_TILE, A_BLOCK), lambda b, j: (b, j, 0))
    ctx_spec = pl.BlockSpec((None, CTX_LEN, A_BLOCK), lambda b, j: (b, LATENT_TILES, 0))
    kv_specs = [prev_spec, cur_spec, next_spec, ctx_spec]
    return pl.pallas_call(
        _window_kernel,
        grid=(batch, n_tiles),
        in_specs=[pl.BlockSpec(memory_space=pltpu.SMEM), _tok_spec(D_MODEL)] + kv_specs + kv_specs,
        out_specs=_tok_spec(D_MODEL),
        out_shape=jax.ShapeDtypeStruct((batch, n_tiles * TOK_TILE, D_MODEL), _BF16),
        scratch_shapes=2 * [pltpu.VMEM((A_STACK, A_BLOCK), _BF16),
                            pltpu.VMEM((A_KEY_TILES, A_STACK, TOK_TILE), _F32),
                            pltpu.VMEM((A_STACK, LANES), _F32),
                            pltpu.VMEM((A_KEY_TILES, A_STACK, TOK_TILE), _BF16),
                            pltpu.VMEM((A_STACK, LANES), _F32)],
        compiler_params=_params(2),
        name="window_attention",
    )(sink, q, k, k, k, k, v, v, v, v)


STACK_ROWS = 2 * TOK_TILE
KEY_TILE = MXU_W
N_KEY_TILES = S_ALL // KEY_TILE
FUSED_TILES = N_KEY_TILES - 1
BODY_TILES = 8
OUT_TILES = 1
EXP_ROWS = STACK_ROWS // FUSED_TILES


def _stack_masked(q, split):
    first = lax.broadcasted_iota(jnp.int32, q.shape, 1) < split
    zero = jnp.zeros((), _BF16)
    return jnp.concatenate([jnp.where(first, q, zero), jnp.where(first, zero, q)], axis=0)


def _tile_rows(t):
    if isinstance(t, int):
        return pl.ds(t * KEY_TILE, KEY_TILE)
    return pl.ds(pl.multiple_of(t * KEY_TILE, KEY_TILE), KEY_TILE)


class _TileScratch(NamedTuple):
    q: Any
    s: Any
    m: Any
    p: Any
    l: Any
    acc: Any


def _qk_tile(tile, t, n, k_ref):
    keys_t = jnp.concatenate([k_ref[t + i] for i in range(n)], axis=1)
    for half in range(2):
        rows = pl.ds(half * TOK_TILE, TOK_TILE)
        s = _dot(tile.q[rows, :], keys_t)
        for i in range(n):
            tile.s[t + i, rows, :] = s[:, i * KEY_TILE:(i + 1) * KEY_TILE]
        lane_blocks = [s[:, lo:lo + LANES] for lo in range(0, n * KEY_TILE, LANES)]
        tile.m[rows, :] = jnp.maximum(tile.m[rows, :], functools.reduce(jnp.maximum, lane_blocks))


def _qk_all(tile, k_ref):
    keys_t = jnp.concatenate([k_ref[t] for t in range(N_KEY_TILES)], axis=1)
    for half in range(2):
        rows = pl.ds(half * TOK_TILE, TOK_TILE)
        s = _dot(tile.q[rows, :], keys_t)
        m = None
        for t in range(N_KEY_TILES):
            st = s[:, t * KEY_TILE:(t + 1) * KEY_TILE]
            tile.s[t, rows, :] = st
            mt = jnp.maximum(st[:, :LANES], st[:, LANES:])
            m = mt if m is None else jnp.maximum(m, mt)
        tile.m[rows, :] = m


def _pv_all(tile, v_ref):
    for half in range(2):
        rows = pl.ds(half * TOK_TILE, TOK_TILE)
        p = jnp.concatenate([tile.p[t, rows, :] for t in range(N_KEY_TILES)], axis=1)
        tile.acc[rows, :] = _dot(p, v_ref[...])


def _reduce_max(tile):
    tile.m[...] = jnp.broadcast_to(jnp.max(tile.m[...], axis=-1, keepdims=True), tile.m.shape)


def _exp_rows(tile, i):
    rs = pl.ds(pl.multiple_of(i * EXP_ROWS, EXP_ROWS), EXP_ROWS)
    m = tile.m[rs, :]
    parts = [None] * PARTIALS
    n = 0
    for c in range(N_KEY_TILES):
        for lo in range(0, KEY_TILE, LANES):
            p = jnp.exp2(tile.s[c, rs, lo:lo + LANES] - m)
            parts[n % PARTIALS] = p if parts[n % PARTIALS] is None else parts[n % PARTIALS] + p
            tile.p[c, rs, lo:lo + LANES] = p.astype(_BF16)
            n += 1
    tile.l[rs, :] = functools.reduce(jnp.add, parts)


def _pv_chunk(tile, t, n, v_ref):
    start = t * KEY_TILE
    if not isinstance(t, int):
        start = pl.multiple_of(start, KEY_TILE)
    vals = v_ref[pl.ds(start, n * KEY_TILE), :]
    for half in range(2):
        rows = pl.ds(half * TOK_TILE, TOK_TILE)
        p = jnp.concatenate([tile.p[t + i, rows, :] for i in range(n)], axis=1)
        tile.acc[rows, :] += _dot(p, vals)


def _fused_stage(scores, exps, outs, k_ref, v_ref):
    if scores is not None:
        scores.m[...] = jnp.full(scores.m.shape, -jnp.inf, _F32)
    if outs is not None:
        outs.acc[...] = jnp.zeros(outs.acc.shape, _F32)

    def body(i, carry):
        base = i * BODY_TILES
        for u in range(BODY_TILES):
            if outs is not None and u % OUT_TILES == 0:
                _pv_chunk(outs, base + u, OUT_TILES, v_ref)
            if exps is not None:
                _exp_rows(exps, base + u)
            if scores is not None:
                _qk_tile(scores, base + u, 1, k_ref)
        return carry

    lax.fori_loop(0, FUSED_TILES // BODY_TILES, body, 0)
    for t in range(FUSED_TILES, N_KEY_TILES):
        if scores is not None:
            _qk_tile(scores, t, 1, k_ref)
        if outs is not None:
            _pv_chunk(outs, t, 1, v_ref)
    if scores is not None:
        _reduce_max(scores)


def _attend_sequence(tiles, q_ref, k_ref, v_ref, o_ref, split, finish):
    even, odd = tiles

    def tile_rows(n):
        start = n * TOK_TILE
        return pl.ds(start if isinstance(n, int) else pl.multiple_of(start, TOK_TILE), TOK_TILE)

    def load(tile, n):
        tile.q[...] = _stack_masked(q_ref[tile_rows(n), :], split)

    def store(tile, n):
        a = tile.acc[...] / jnp.sum(tile.l[...], axis=-1, keepdims=True)
        o_ref[tile_rows(n), :] = finish(a)

    load(even, 0)
    _qk_all(even, k_ref)
    _reduce_max(even)
    load(odd, 1)
    _fused_stage(odd, even, None, k_ref, v_ref)

    def two_tiles(i, carry):
        n = 2 * i
        load(even, n + 2)
        _fused_stage(even, odd, even, k_ref, v_ref)
        store(even, n)
        load(odd, n + 3)
        _fused_stage(odd, even, odd, k_ref, v_ref)
        store(odd, n + 1)
        return carry

    lax.fori_loop(0, LATENT_TILES // 2 - 1, two_tiles, 0)
    _fused_stage(None, odd, even, k_ref, v_ref)
    store(even, LATENT_TILES - 2)
    _pv_all(odd, v_ref)
    store(odd, LATENT_TILES - 1)


def _pair_scratch(qk_width, v_width):
    one = [pltpu.VMEM((STACK_ROWS, qk_width), _BF16),
           pltpu.VMEM((N_KEY_TILES, STACK_ROWS, KEY_TILE), _F32),
           pltpu.VMEM((STACK_ROWS, LANES), _F32),
           pltpu.VMEM((N_KEY_TILES, STACK_ROWS, KEY_TILE), _BF16),
           pltpu.VMEM((STACK_ROWS, LANES), _F32),
           pltpu.VMEM((STACK_ROWS, v_width), _F32)]
    return one + one


def _pair_tiles(scratch):
    n = len(_TileScratch._fields)
    return _TileScratch(*scratch[:n]), _TileScratch(*scratch[n:])


def _attend_context(q, k_ref, v_ref):
    s = _dot(q, k_ref[...])
    p = jnp.exp2(s - jnp.max(s, axis=-1, keepdims=True))
    return _dot(p.astype(_BF16), v_ref[...]) / jnp.sum(p, axis=-1, keepdims=True)


def _dense_attention(latent_kernel, context_kernel, params, q, k, v, heads, qk_width, v_width,
                     with_context, name):
    assert KEY_TILE == TOK_TILE == CTX_LEN
    batch = q.shape[0]
    pair = 2 * TOK_TILE
    n_rows = S_ALL if with_context else SEQ
    out_shape = jax.ShapeDtypeStruct((batch, n_rows, heads * v_width), _BF16)

    param_specs = [pl.BlockSpec(p.shape, lambda *_, nd=p.ndim: (0,) * nd) for p in params]

    out = pl.pallas_call(
        latent_kernel,
        grid=(batch, heads),
        in_specs=param_specs + [
            pl.BlockSpec((None, SEQ, qk_width), lambda b, h: (b, 0, h)),
            pl.BlockSpec((None, N_KEY_TILES, qk_width, KEY_TILE), lambda b, h: (b, 0, h, 0)),
            pl.BlockSpec((None, S_ALL, v_width), lambda b, h: (b, 0, h))],
        out_specs=pl.BlockSpec((None, SEQ, v_width), lambda b, h: (b, 0, h)),
        out_shape=out_shape,
        scratch_shapes=_pair_scratch(qk_width, v_width),
        compiler_params=_params(2),
        name=name,
    )(*params, q, k, v)
    if not with_context:
        return out
    ctx_block = SEQ // CTX_LEN
    return pl.pallas_call(
        context_kernel,
        grid=(batch, heads),
        in_specs=param_specs + [
            pl.BlockSpec((None, CTX_LEN, qk_width), lambda b, h: (b, ctx_block, h)),
            pl.BlockSpec((None, None, qk_width, KEY_TILE), lambda b, h: (b, ctx_block, h, 0)),
            pl.BlockSpec((None, CTX_LEN, v_width), lambda b, h: (b, ctx_block, h)),
            pl.BlockSpec(memory_space=pl.ANY)],
        out_specs=pl.BlockSpec((None, CTX_LEN, v_width), lambda b, h: (b, ctx_block, h)),
        out_shape=out_shape,
        input_output_aliases={len(params) + 3: 0},
        compiler_params=_params(2),
        name=name + "_context",
    )(*params, q, k, v, out)


def _diff_finish(a, lam_ref, subln_ref, lam_init):
    lam = lam_ref[...]
    lam_full = (jnp.exp(jnp.sum(lam[0:1] * lam[1:2], axis=-1, keepdims=True))
                - jnp.exp(jnp.sum(lam[2:3] * lam[3:4], axis=-1, keepdims=True)) + lam_init)
    o = a[:TOK_TILE] - lam_full * a[TOK_TILE:]
    return (_rms(o) * subln_ref[...] * (1.0 - lam_init)).astype(_BF16)


def _diff_latent_kernel(lam_ref, subln_ref, q_ref, k_ref, v_ref, o_ref, *scratch, lam_init):
    finish = functools.partial(_diff_finish, lam_ref=lam_ref, subln_ref=subln_ref, lam_init=lam_init)
    _attend_sequence(_pair_tiles(scratch), q_ref, k_ref, v_ref, o_ref, B_HEAD_DIM, finish)


def _diff_context_kernel(lam_ref, subln_ref, q_ref, k_ref, v_ref, buf_ref, o_ref, *, lam_init):
    del buf_ref
    a = _attend_context(_stack_masked(q_ref[...], B_HEAD_DIM), k_ref, v_ref)
    o_ref[...] = _diff_finish(a, lam_ref, subln_ref, lam_init)


def _diff_attention(q, k, v, lam, subln, lam_init, with_context):
    head_w = 2 * B_HEAD_DIM
    return _dense_attention(functools.partial(_diff_latent_kernel, lam_init=lam_init),
                            functools.partial(_diff_context_kernel, lam_init=lam_init),
                            (lam, subln), q, k, v, B_HEADS, head_w, head_w, with_context,
                            "diff_attention")


def _mla_finish(a):
    first_v = lax.broadcasted_iota(jnp.int32, (TOK_TILE, 2 * C_V), 1) < C_V
    return jnp.where(first_v, a[:TOK_TILE], a[TOK_TILE:]).astype(_BF16)


def _mla_latent_kernel(q_ref, k_ref, v_ref, o_ref, *scratch):
    _attend_sequence(_pair_tiles(scratch), q_ref, k_ref, v_ref, o_ref, C_QK_PAD, _mla_finish)


def _mla_context_kernel(q_ref, k_ref, v_ref, buf_ref, o_ref):
    del buf_ref
    o_ref[...] = _mla_finish(_attend_context(_stack_masked(q_ref[...], C_QK_PAD), k_ref, v_ref))


def _mla_attention(q, k, v, with_context):
    return _dense_attention(_mla_latent_kernel, _mla_context_kernel, (), q, k, v, C_HEADS // 2,
                            2 * C_QK_PAD, 2 * C_V, with_context, "mla_attention")


def _post_kernel(a_ref, h_ref, mod_ref, g_ref, wo_ref, w1_ref, w2_ref, o_ref):
    y = _dot(a_ref[...], wo_ref[...])
    h1 = h_ref[...] + mod_ref[2:3, :] * (_rms(y) * g_ref[1:2, :])
    u = (_rms(h1) * g_ref[2:3, :]) * (1.0 + mod_ref[4:5, :]) + mod_ref[3:4, :]
    t = jnp.maximum(_dot(u.astype(_BF16), w1_ref[...]), 0.0)
    z = _dot((t * t).astype(_BF16), w2_ref[...])
    o_ref[...] = h1 + mod_ref[5:6, :] * (_rms(z) * g_ref[3:4, :])


def _post(a, h, mod_all, norms, w_o, w1, w2, n_tiles):
    batch = h.shape[0]
    return pl.pallas_call(
        _post_kernel,
        grid=(batch, n_tiles),
        in_specs=[_tok_spec(D_MODEL), _tok_spec(D_MODEL), _mod_spec(), _const_spec((4, D_MODEL)),
                  _const_spec((D_MODEL, D_MODEL)), _const_spec((D_MODEL, D_FF)),
                  _const_spec((D_FF, D_MODEL))],
        out_specs=_tok_spec(D_MODEL),
        out_shape=jax.ShapeDtypeStruct((batch, n_tiles * TOK_TILE, D_MODEL), _F32),
        compiler_params=_params(2),
        name="post",
    )(a, h, mod_all, norms, w_o, w1, w2)


def _a_head_perm():
    new = np.arange(A_HEADS * A_HEAD_DIM)
    r, g, d = new // A_BLOCK, (new % A_BLOCK) // A_HEAD_DIM, new % A_HEAD_DIM
    return (g * A_GROUP + r) * A_HEAD_DIM + d


def _layer_window(h, mod_all, norms, w_qkv, sink, w_o, n_tiles, tables):
    nq = A_HEADS * A_HEAD_DIM
    nk = A_KV_HEADS * A_HEAD_DIM
    perm = _a_head_perm()
    w = jnp.concatenate([w_qkv[:, :nq][:, perm], w_qkv[:, nq:]], axis=1).astype(_BF16)
    q, k, v = _qkv_proj(h, mod_all, norms, w, tables, nq, nk, nk, A_HEAD_DIM ** -0.5 * LOG2E, False)
    a = _window_attention(q, k, v, sink, n_tiles)
    return a, w_o[perm, :]


def _layer_diff(h, mod_all, norms, w_qkv, lam, subln, w_o, layer_idx, n_tiles, tables):
    n = B_HEADS * 2 * B_HEAD_DIM
    lam_init = 0.8 - 0.6 * math.exp(-0.3 * layer_idx)
    q, k, v = _qkv_proj(h, mod_all, norms, w_qkv.astype(_BF16), tables, n, n, n,
                        B_HEAD_DIM ** -0.5 * LOG2E, True)
    a = _diff_attention(q, k, v, lam, subln.reshape(1, -1), lam_init, n_tiles == TILES_PER_SAMPLE)
    return a, w_o


def _layer_mla(h, mod_all, norms, w_in, q_norm, kv_norm, w_uq, w_ukv, w_o, n_tiles, tables):
    dqk = C_NOPE + C_ROPE
    zeros = functools.partial(jnp.zeros, dtype=_F32)
    w_in_p = jnp.concatenate(
        [w_in[:, :C_Q_LORA + C_KV_LORA], zeros((D_MODEL, C_NOPE)), w_in[:, C_Q_LORA + C_KV_LORA:],
         zeros((D_MODEL, C_QK_PAD - dqk))], axis=1).astype(_BF16)
    w_uq_p = jnp.pad(w_uq.reshape(C_Q_LORA, C_HEADS, dqk), ((0, 0), (0, 0), (0, C_QK_PAD - dqk)))
    w_uq_p = w_uq_p.reshape(C_Q_LORA, C_HEADS * C_QK_PAD).astype(_BF16)
    w_ukv_h = w_ukv.reshape(C_KV_LORA, C_HEADS, C_NOPE + C_V)
    w_uk_p = jnp.pad(w_ukv_h[:, :, :C_NOPE], ((0, 0), (0, 0), (0, C_QK_PAD - C_NOPE)))
    w_uk_p = w_uk_p.reshape(C_KV_LORA, C_HEADS * C_QK_PAD).astype(_BF16)
    w_uv = w_ukv_h[:, :, C_NOPE:].reshape(C_KV_LORA, C_HEADS * C_V).astype(_BF16)
    q, k, v = _mla_proj(h, mod_all, norms, w_in_p, q_norm.reshape(1, -1), kv_norm.reshape(1, -1),
                        w_uq_p, w_uk_p, w_uv, tables, dqk ** -0.5 * LOG2E)
    a = _mla_attention(q, k, v, n_tiles == TILES_PER_SAMPLE)
    return a, w_o


def kernel(x, c, ctx, c_ctx, l0_ada_w, l0_ada_b, l0_norms, l0_w_qkv, l0_sink, l0_w_o, l0_mlp_w1, l0_mlp_w2, l1_ada_w, l1_ada_b, l1_norms, l1_w_qkv, l1_lambda, l1_subln, l1_w_o, l1_mlp_w1, l1_mlp_w2, l2_ada_w, l2_ada_b, l2_norms, l2_w_in, l2_q_norm, l2_kv_norm, l2_w_uq, l2_w_ukv, l2_w_o, l2_mlp_w1, l2_mlp_w2, l3_ada_w, l3_ada_b, l3_norms, l3_w_qkv, l3_sink, l3_w_o, l3_mlp_w1, l3_mlp_w2):
    batch = x.shape[0]
    assert x.shape == (batch, SEQ, D_MODEL) and ctx.shape == (batch, CTX_LEN, D_MODEL) and batch < 16
    layers = [
        (l0_ada_w, l0_ada_b, l0_norms, l0_w_o, l0_mlp_w1, l0_mlp_w2),
        (l1_ada_w, l1_ada_b, l1_norms, l1_w_o, l1_mlp_w1, l1_mlp_w2),
        (l2_ada_w, l2_ada_b, l2_norms, l2_w_o, l2_mlp_w1, l2_mlp_w2),
        (l3_ada_w, l3_ada_b, l3_norms, l3_w_o, l3_mlp_w1, l3_mlp_w2),
    ]
    tables_ab = _rope_tables(A_HEAD_DIM, 0, A_HEAD_DIM)
    tables_c = _rope_tables(C_ROPE, C_NOPE, LANES)
    cc = jnp.zeros((16, D_MODEL), _F32).at[:batch].set(c).at[batch].set(c_ctx)
    h = jnp.concatenate([x, ctx], axis=1)
    depth = len(layers)
    for i, (ada_w, ada_b, norms, w_o, w1, w2) in enumerate(layers):
        n_tiles = LATENT_TILES if i == depth - 1 else TILES_PER_SAMPLE
        mod = _ada(cc, ada_w, ada_b)
        mod_lat = mod[:batch].reshape(batch, 1, 6, D_MODEL)
        mod_ctx = jnp.broadcast_to(mod[batch].reshape(1, 1, 6, D_MODEL), (batch, 1, 6, D_MODEL))
        mod_all = jnp.concatenate([mod_lat, mod_ctx], axis=1)
        if i == 0:
            a, w_o = _layer_window(h, mod_all, norms, l0_w_qkv, l0_sink, w_o, n_tiles, tables_ab)
        elif i == 1:
            a, w_o = _layer_diff(h, mod_all, norms, l1_w_qkv, l1_lambda, l1_subln, w_o, i, n_tiles,
                                 tables_ab)
        elif i == 2:
            a, w_o = _layer_mla(h, mod_all, norms, l2_w_in, l2_q_norm, l2_kv_norm, l2_w_uq, l2_w_ukv,
                                w_o, n_tiles, tables_c)
        else:
            a, w_o = _layer_window(h, mod_all, norms, l3_w_qkv, l3_sink, w_o, n_tiles, tables_ab)
        h = _post(a, h, mod_all, norms, w_o.astype(_BF16), w1.astype(_BF16), w2.astype(_BF16), n_tiles)
    return h
```
